```python
import jax, jax.numpy as jnp
from jax import lax
import numpy as np

D_MODEL = 1024
BATCH = 32
SEQ = 2048
DEPTH = 1
DEC_BATCH = 4
DEC_SEQ = 8192
PAST_LEN = 128

MLA_HEADS = 8
Q_LORA = 256
KV_LORA = 128
NOPE_DIM = 64
ROPE_DIM = 32
MLA_V_DIM = 64
Q_BLOCK = 128
RET_HEADS = 8
RET_QK_DIM = 64
RET_V_DIM = 128
CHUNK = 128
D_FF = 4 * D_MODEL
ROPE_THETA = 10000.0
EPS = 1e-6

SPLITS = (Q_LORA, KV_LORA, ROPE_DIM,
          RET_HEADS * RET_QK_DIM, RET_HEADS * RET_QK_DIM,
          RET_HEADS * RET_V_DIM, RET_HEADS * RET_V_DIM,
          D_MODEL, D_MODEL)
D_IN = sum(SPLITS)

kernel_name = "hybrid_mla_retention_encoder"


def rms_norm(x, g=None):
    xf = x.astype(jnp.float32)
    y = xf * lax.rsqrt(jnp.mean(xf * xf, axis=-1, keepdims=True) + EPS)
    if g is not None:
        y = y * g.astype(jnp.float32)
    return y.astype(x.dtype)


def rope_tables(seq, dim):
    inv = 1.0 / (ROPE_THETA ** (jnp.arange(0, dim, 2, dtype=jnp.float32) / dim))
    ang = jnp.arange(seq, dtype=jnp.float32)[:, None] * inv[None, :]
    return jnp.cos(ang), jnp.sin(ang)


def apply_rope(x, cos, sin):
    half = x.shape[-1] // 2
    x1 = x[..., :half].astype(jnp.float32)
    x2 = x[..., half:].astype(jnp.float32)
    return jnp.concatenate([x1 * cos - x2 * sin, x1 * sin + x2 * cos], axis=-1).astype(x.dtype)


def mla_attention(q_nope, q_rope, k_nope, k_rope, v):
    B, S, H, _ = q_nope.shape
    nb = S // Q_BLOCK
    scale = (NOPE_DIM + ROPE_DIM) ** -0.5

    def block(args):
        qn, qr = args
        s = (jnp.einsum('bqhd,bkhd->bhqk', qn, k_nope)
             + jnp.einsum('bqhd,bkd->bhqk', qr, k_rope))
        p = jax.nn.softmax(s.astype(jnp.float32) * scale, axis=-1).astype(v.dtype)
        return jnp.einsum('bhqk,bkhd->bqhd', p, v)

    qn_b = q_nope.reshape(B, nb, Q_BLOCK, H, NOPE_DIM).transpose(1, 0, 2, 3, 4)
    qr_b = q_rope.reshape(B, nb, Q_BLOCK, H, ROPE_DIM).transpose(1, 0, 2, 3, 4)
    out = lax.map(block, (qn_b, qr_b))
    return out.transpose(1, 0, 2, 3, 4).reshape(B, S, H * MLA_V_DIM)


def retention_dir(q, k, v, log_gamma, inclusive):
    B, H, S, Dk = q.shape
    Dv = v.shape[-1]
    n = S // CHUNK
    lg = log_gamma.astype(jnp.float32)
    idx = jnp.arange(CHUNK, dtype=jnp.float32)
    diff = idx[:, None] - idx[None, :]
    mask = (diff >= 0) if inclusive else (diff > 0)
    intra = jnp.where(mask[None], jnp.exp(lg[:, None, None] * jnp.maximum(diff, 0.0)[None]), 0.0)
    q_dec = jnp.exp(lg[:, None] * (idx[None, :] + 1.0))
    k_dec = jnp.exp(lg[:, None] * (CHUNK - 1.0 - idx[None, :]))
    chunk_dec = jnp.exp(lg * CHUNK)

    qc = q.reshape(B, H, n, CHUNK, Dk)
    kc = k.reshape(B, H, n, CHUNK, Dk)
    vc = v.reshape(B, H, n, CHUNK, Dv)

    s = jnp.einsum('bhnid,bhnjd->bhnij', qc, kc) * intra[:, None]
    o_intra = jnp.einsum('bhnij,bhnjv->bhniv', s, vc)

    kv = jnp.einsum('bhnjd,bhnjv->nbhdv', kc * k_dec[:, None, :, None], vc).astype(jnp.float32)

    def step(state, kv_n):
        return state * chunk_dec[None, :, None, None] + kv_n, state

    _, states = lax.scan(step, jnp.zeros((B, H, Dk, Dv), jnp.float32), kv)
    o_cross = jnp.einsum('bhnid,nbhdv->bhniv', qc * q_dec[:, None, :, None], states)
    return (o_intra + o_cross).reshape(B, H, S, Dv).astype(v.dtype)


def mixer(h, w_in, g_q_norm, w_q_up, g_kv_norm, w_kv_up, w_branch_a,
          ret_log_decay_fwd, ret_log_decay_bwd, w_branch_b, w_out):
    B, S, _ = h.shape
    offsets = np.cumsum(np.array(SPLITS))[:-1].tolist()
    c_q, c_kv, k_r, rq, rk, rv, rg, ga, gb = jnp.split(h @ w_in, offsets, axis=-1)

    cos_m, sin_m = rope_tables(S, ROPE_DIM)
    q = (rms_norm(c_q, g_q_norm) @ w_q_up).reshape(B, S, MLA_HEADS, NOPE_DIM + ROPE_DIM)
    q_nope, q_rope = q[..., :NOPE_DIM], q[..., NOPE_DIM:]
    q_rope = apply_rope(q_rope, cos_m[:, None, :], sin_m[:, None, :])
    kv = (rms_norm(c_kv, g_kv_norm) @ w_kv_up).reshape(B, S, MLA_HEADS, NOPE_DIM + MLA_V_DIM)
    k_nope, v_a = kv[..., :NOPE_DIM], kv[..., NOPE_DIM:]
    k_rope = apply_rope(k_r, cos_m, sin_m)
    a = mla_attention(q_nope, q_rope, k_nope, k_rope, v_a) @ w_branch_a

    cos_r, sin_r = rope_tables(S, RET_QK_DIM)
    rq = apply_rope(rq.reshape(B, S, RET_HEADS, RET_QK_DIM), cos_r[:, None, :], sin_r[:, None, :])
    rk = apply_rope(rk.reshape(B, S, RET_HEADS, RET_QK_DIM), cos_r[:, None, :], sin_r[:, None, :])
    rq = rq.transpose(0, 2, 1, 3)
    rk = (rk * (RET_QK_DIM ** -0.5)).transpose(0, 2, 1, 3)
    rv = rv.reshape(B, S, RET_HEADS, RET_V_DIM).transpose(0, 2, 1, 3)
    o_f = retention_dir(rq, rk, rv, ret_log_decay_fwd, True)
    o_b = jnp.flip(retention_dir(jnp.flip(rq, 2), jnp.flip(rk, 2), jnp.flip(rv, 2),
                                 ret_log_decay_bwd, False), 2)
    o = rms_norm((o_f + o_b).transpose(0, 2, 1, 3))
    o = o.reshape(B, S, RET_HEADS * RET_V_DIM) * jax.nn.silu(rg)
    b = o @ w_branch_b

    merged = jax.nn.sigmoid(ga) * a + jax.nn.sigmoid(gb) * b
    return merged @ w_out


def trunk(x, g_pre_mix, w_in, g_q_norm, w_q_up, g_kv_norm, w_kv_up, w_branch_a,
          ret_log_decay_fwd, ret_log_decay_bwd, w_branch_b, w_out, g_post_mix,
          g_pre_mlp, w_up, w_down, g_post_mlp):
    for l in range(DEPTH):
        h = rms_norm(x, g_pre_mix[l])
        m = mixer(h, w_in[l], g_q_norm[l], w_q_up[l], g_kv_norm[l], w_kv_up[l], w_branch_a[l],
                  ret_log_decay_fwd[l], ret_log_decay_bwd[l], w_branch_b[l], w_out[l])
        x = x + rms_norm(m, g_post_mix[l])
        h = rms_norm(x, g_pre_mlp[l])
        u = jnp.square(jax.nn.relu(h @ w_up[l]))
        x = x + rms_norm(u @ w_down[l], g_post_mlp[l])
    return x


def setup_inputs(seed: int = 0) -> dict:
    key = jax.random.key(seed)
    ks = jax.random.split(key, 24)

    def w(k, fan_in, fan_out):
        return jax.random.normal(k, (DEPTH, fan_in, fan_out), jnp.float32) * fan_in ** -0.5

    def gain(k, dim):
        return 1.0 + 0.05 * jax.random.normal(k, (DEPTH, dim), jnp.float32)

    base = jnp.log1p(-jnp.exp2(-5.0 - jnp.arange(RET_HEADS, dtype=jnp.float32)))
    dec_f = base[None] * (1.0 + 0.05 * jax.random.normal(ks[20], (DEPTH, RET_HEADS), jnp.float32))
    dec_b = base[None] * (1.0 + 0.05 * jax.random.normal(ks[21], (DEPTH, RET_HEADS), jnp.float32))
    return {
        "x_prompt": jax.random.normal(ks[0], (BATCH, SEQ, D_MODEL), jnp.float32),
        "x_sample": jax.random.normal(ks[1], (DEC_BATCH, DEC_SEQ, D_MODEL), jnp.float32),
        "g_pre_mix": gain(ks[2], D_MODEL),
        "w_in": w(ks[3], D_MODEL, D_IN),
        "g_q_norm": gain(ks[4], Q_LORA),
        "w_q_up": w(ks[5], Q_LORA, MLA_HEADS * (NOPE_DIM + ROPE_DIM)),
        "g_kv_norm": gain(ks[6], KV_LORA),
        "w_kv_up": w(ks[7], KV_LORA, MLA_HEADS * (NOPE_DIM + MLA_V_DIM)),
        "w_branch_a": w(ks[8], MLA_HEADS * MLA_V_DIM, D_MODEL),
        "ret_log_decay_fwd": dec_f,
        "ret_log_decay_bwd": dec_b,
        "w_branch_b": w(ks[9], RET_HEADS * RET_V_DIM, D_MODEL),
        "w_out": w(ks[10], D_MODEL, D_MODEL),
        "g_post_mix": gain(ks[11], D_MODEL),
        "g_pre_mlp": gain(ks[12], D_MODEL),
        "w_up": w(ks[13], D_MODEL, D_FF),
        "w_down": w(ks[14], D_FF, D_MODEL),
        "g_post_mlp": gain(ks[15], D_MODEL),
    }


def reference(x_prompt, x_sample, g_pre_mix, w_in, g_q_norm, w_q_up, g_kv_norm, w_kv_up,
              w_branch_a, ret_log_decay_fwd, ret_log_decay_bwd, w_branch_b, w_out, g_post_mix,
              g_pre_mlp, w_up, w_down, g_post_mlp):
    y_prompt = trunk(x_prompt, g_pre_mix, w_in, g_q_norm, w_q_up, g_kv_norm, w_kv_up, w_branch_a,
                     ret_log_decay_fwd, ret_log_decay_bwd, w_branch_b, w_out, g_post_mix,
                     g_pre_mlp, w_up, w_down, g_post_mlp)
    y_sample = trunk(x_sample, g_pre_mix, w_in, g_q_norm, w_q_up, g_kv_norm, w_kv_up, w_branch_a,
                     ret_log_decay_fwd, ret_log_decay_bwd, w_branch_b, w_out, g_post_mix,
                     g_pre_mlp, w_up, w_down, g_post_mlp)
    return (y_prompt, y_sample)
```

```python
import functools

import jax
import jax.numpy as jnp
import numpy as np
from jax import lax
from jax.experimental import pallas as pl
from jax.experimental.pallas import tpu as pltpu

F32 = jnp.float32
BF16 = jnp.bfloat16

D_MODEL = 1024
MLA_HEADS = 8
Q_LORA = 256
KV_LORA = 128
NOPE_DIM = 64
ROPE_DIM = 32
MLA_V_DIM = 64
RET_HEADS = 8
RET_QK_DIM = 64
RET_V_DIM = 128
CHUNK = 128
D_FF = 4 * D_MODEL
ROPE_THETA = 10000.0
EPS = 1e-6

LANES = 128
QK_PAD = LANES
SMALL_N = Q_LORA + KV_LORA + 2 * LANES
VMEM_LIMIT = 56 * 1024 * 1024

TM_PRE = 512
TM_POST = 512
TQ_ATTN = 512
TKV_ATTN = 512


def _dot(a, b):
    return jnp.dot(a, b, preferred_element_type=F32)


def _dot_nt(a, b):
    return lax.dot_general(a, b, (((1,), (1,)), ((), ())), preferred_element_type=F32)


def _dot_tn(a, b):
    return lax.dot_general(a, b, (((0,), (0,)), ((), ())), preferred_element_type=F32)


def _rms(x, g=None):
    y = x * lax.rsqrt(jnp.mean(x * x, axis=-1, keepdims=True) + EPS)
    if g is not None:
        y = y * g
    return y


def _const_spec(shape):
    nd = len(shape)
    return pl.BlockSpec(shape, lambda *_: (0,) * nd, pipeline_mode=pl.Buffered(1))


def _pre_kernel(x_ref, g_ref, wsm_ref, wrq_ref, wrk_ref, wrv_ref, wrg_ref, wga_ref, wgb_ref,
                gq_ref, wqn_ref, wqr_ref, gkv_ref, wk_ref, wv_ref,
                cos_t_ref, sin_t_ref, cos_k_ref, sin_k_ref, cos_r_ref, sin_r_ref,
                qt_ref, k_ref, vt_ref, rq_ref, rk_ref, rv_ref, rg_ref, sa_ref, sb_ref):
    tm = x_ref.shape[1]
    h = _rms(x_ref[0], g_ref[...]).astype(BF16)

    sm = _dot(h, wsm_ref[...])
    cqn = _rms(sm[:, :Q_LORA], gq_ref[...]).astype(BF16)
    ckvn = _rms(sm[:, Q_LORA:Q_LORA + KV_LORA], gkv_ref[...]).astype(BF16)
    kr_a = sm[:, Q_LORA + KV_LORA:Q_LORA + KV_LORA + LANES]
    kr_b = sm[:, Q_LORA + KV_LORA + LANES:]
    k_rope = (kr_a * cos_k_ref[...] + kr_b * sin_k_ref[...]).astype(BF16)

    scale = (NOPE_DIM + ROPE_DIM) ** -0.5
    qn = (_dot_nt(wqn_ref[...], cqn) * scale).reshape(MLA_HEADS, NOPE_DIM, tm)
    qr = (_dot_nt(wqr_ref[...], cqn) * scale).reshape(MLA_HEADS, ROPE_DIM, tm)
    half = ROPE_DIM // 2
    x1, x2 = qr[:, :half], qr[:, half:]
    cos_t, sin_t = cos_t_ref[...][None], sin_t_ref[...][None]
    qt = jnp.concatenate(
        [qn, x1 * cos_t - x2 * sin_t, x1 * sin_t + x2 * cos_t,
         jnp.zeros((MLA_HEADS, QK_PAD - NOPE_DIM - ROPE_DIM, tm), F32)], axis=1)
    qt_ref[0] = qt.astype(BF16)

    k_ref[0] = _dot(jnp.concatenate([ckvn, k_rope], axis=1), wk_ref[...]).astype(BF16)
    vt_ref[0] = _dot_nt(wv_ref[...], ckvn).reshape(MLA_HEADS, MLA_V_DIM, tm).astype(BF16)

    cos_r, sin_r = cos_r_ref[...], sin_r_ref[...]
    rq = _dot(h, wrq_ref[...])
    rk = _dot(h, wrk_ref[...]) * (RET_QK_DIM ** -0.5)
    for p in range(RET_HEADS // 2):
        sl = slice(p * LANES, (p + 1) * LANES)
        bq, bk = rq[:, sl], rk[:, sl]
        rq_ref[0, :, sl] = (bq * cos_r + pltpu.roll(bq, LANES // 2, 1) * sin_r).astype(BF16)
        rk_ref[0, :, sl] = (bk * cos_r + pltpu.roll(bk, LANES // 2, 1) * sin_r).astype(BF16)

    rv_ref[0] = _dot(h, wrv_ref[...]).astype(BF16)
    rg = _dot(h, wrg_ref[...])
    rg_ref[0] = (rg * jax.nn.sigmoid(rg)).astype(BF16)
    sa_ref[0] = jax.nn.sigmoid(_dot(h, wga_ref[...])).astype(BF16)
    sb_ref[0] = jax.nn.sigmoid(_dot(h, wgb_ref[...])).astype(BF16)


def _pre_stage(x, w):
    B, S, D = x.shape
    tm = min(TM_PRE, S)
    grid = (B, S // tm)
    tok = lambda n: pl.BlockSpec((1, tm, n), lambda b, i: (b, i, 0))
    tab = lambda n: pl.BlockSpec((tm, n), lambda b, i: (i, 0))
    tab_t = pl.BlockSpec((ROPE_DIM // 2, tm), lambda b, i: (0, i))
    head_t = lambda n: pl.BlockSpec((1, MLA_HEADS, n, tm), lambda b, i: (b, 0, 0, i))
    consts = [w["g_pre_mix"], w["w_small"], w["w_rq"], w["w_rk"], w["w_rv"], w["w_rg"], w["w_ga"], w["w_gb"],
              w["g_q_norm"], w["w_qn_t"], w["w_qr_t"], w["g_kv_norm"], w["w_k"], w["w_v_t"]]
    cos_t, sin_t, cos_k, sin_k, cos_r, sin_r = _rope_tables(S)
    out_shape = [
        jax.ShapeDtypeStruct((B, MLA_HEADS, QK_PAD, S), BF16),
        jax.ShapeDtypeStruct((B, S, MLA_HEADS * QK_PAD), BF16),
        jax.ShapeDtypeStruct((B, MLA_HEADS, MLA_V_DIM, S), BF16),
        jax.ShapeDtypeStruct((B, S, RET_HEADS * RET_QK_DIM), BF16),
        jax.ShapeDtypeStruct((B, S, RET_HEADS * RET_QK_DIM), BF16),
        jax.ShapeDtypeStruct((B, S, RET_HEADS * RET_V_DIM), BF16),
        jax.ShapeDtypeStruct((B, S, RET_HEADS * RET_V_DIM), BF16),
        jax.ShapeDtypeStruct((B, S, D), BF16),
        jax.ShapeDtypeStruct((B, S, D), BF16),
    ]
    out_specs = [head_t(QK_PAD), tok(MLA_HEADS * QK_PAD), head_t(MLA_V_DIM),
                 tok(RET_HEADS * RET_QK_DIM), tok(RET_HEADS * RET_QK_DIM),
                 tok(RET_HEADS * RET_V_DIM), tok(RET_HEADS * RET_V_DIM), tok(D), tok(D)]
    return pl.pallas_call(
        _pre_kernel,
        grid=grid,
        in_specs=[tok(D)] + [_const_spec(c.shape) for c in consts]
                 + [tab_t, tab_t, tab(LANES), tab(LANES), tab(LANES), tab(LANES)],
        out_specs=out_specs,
        out_shape=out_shape,
        compiler_params=pltpu.CompilerParams(
            dimension_semantics=("parallel", "parallel"), vmem_limit_bytes=VMEM_LIMIT),
        name="pre_stage",
    )(x, *consts, cos_t, sin_t, cos_k, sin_k, cos_r, sin_r)


def _rope_angles(seq, dim):
    inv = 1.0 / (ROPE_THETA ** (jnp.arange(0, dim, 2, dtype=F32) / dim))
    ang = jnp.arange(seq, dtype=F32)[:, None] * inv[None, :]
    return jnp.cos(ang), jnp.sin(ang)


def _rope_tables(S):
    cos_m, sin_m = _rope_angles(S, ROPE_DIM)
    zpad = jnp.zeros((S, LANES - ROPE_DIM), F32)
    cos_k = jnp.concatenate([cos_m, cos_m, zpad], axis=1)
    sin_k = jnp.concatenate([-sin_m, sin_m, zpad], axis=1)
    cos_r, sin_r = _rope_angles(S, RET_QK_DIM)
    cos_r4 = jnp.concatenate([cos_r] * 4, axis=1)
    sin_r4 = jnp.concatenate([-sin_r, -sin_r, sin_r, sin_r], axis=1)
    return cos_m.T, sin_m.T, cos_k, sin_k, cos_r4, sin_r4


def _attn_kernel(qt_ref, k_ref, vt_ref, o_ref, *, tkv):
    qt = qt_ref[0, 0]
    tq = qt.shape[1]
    n_kv = k_ref.shape[1] // tkv

    def body(c, carry):
        m, l, acc = carry
        start = pl.multiple_of(c * tkv, tkv)
        s = _dot(k_ref[0, pl.ds(start, tkv), :], qt)
        m_new = jnp.maximum(m, jnp.max(s, axis=0, keepdims=True))
        alpha = jnp.exp(m - m_new)
        p = jnp.exp(s - m_new)
        l = alpha * l + jnp.sum(p, axis=0, keepdims=True)
        acc = alpha * acc + _dot(vt_ref[0, 0, :, pl.ds(start, tkv)], p.astype(BF16))
        return m_new, l, acc

    init = (jnp.full((1, tq), -jnp.inf, F32), jnp.zeros((1, tq), F32), jnp.zeros((MLA_V_DIM, tq), F32))
    _, l, acc = lax.fori_loop(0, n_kv, body, init)
    o_ref[0, 0] = (acc / l).astype(BF16)


def _attn_stage(qt, k, vt):
    B, H, _, S = qt.shape
    tq = min(TQ_ATTN, S)
    tkv = min(TKV_ATTN, S)
    return pl.pallas_call(
        functools.partial(_attn_kernel, tkv=tkv),
        grid=(B, H, S // tq),
        in_specs=[
            pl.BlockSpec((1, 1, QK_PAD, tq), lambda b, h, i: (b, h, 0, i)),
            pl.BlockSpec((1, S, QK_PAD), lambda b, h, i: (b, 0, h)),
            pl.BlockSpec((1, 1, MLA_V_DIM, S), lambda b, h, i: (b, h, 0, 0)),
        ],
        out_specs=pl.BlockSpec((1, 1, MLA_V_DIM, tq), lambda b, h, i: (b, h, 0, i)),
        out_shape=jax.ShapeDtypeStruct((B, H, MLA_V_DIM, S), BF16),
        compiler_params=pltpu.CompilerParams(
            dimension_semantics=("parallel", "parallel", "arbitrary"), vmem_limit_bytes=VMEM_LIMIT),
        name="attn_stage",
    )(qt, k, vt)


def _ret_kernel(lgf_ref, lgb_ref, rq_ref, rk_ref, rv_ref, rg_ref, o_ref, sb_scr):
    C = CHUNK
    n_chunks = rq_ref.shape[1] // C
    pair = pl.program_id(1)
    lane = lax.broadcasted_iota(jnp.int32, (1, LANES), 1)
    row = lax.broadcasted_iota(jnp.int32, (C, LANES), 0).astype(F32)
    i_idx = lax.broadcasted_iota(jnp.int32, (C, C), 0).astype(F32)
    j_idx = lax.broadcasted_iota(jnp.int32, (C, C), 1).astype(F32)
    diff = i_idx - j_idx

    for hh in range(2):
        head = 2 * pair + hh
        lgf, lgb = lgf_ref[head], lgb_ref[head]
        hmask = (lane // (RET_QK_DIM // 2)) % 2 == hh
        vsl = slice(hh * RET_V_DIM, (hh + 1) * RET_V_DIM)
        decay = jnp.where(diff >= 0, jnp.exp(lgf * jnp.maximum(diff, 0.0)), jnp.exp(lgb * jnp.maximum(-diff, 0.0)))
        q_dec_f = jnp.exp(lgf * (row + 1.0))
        q_dec_b = jnp.exp(lgb * (C - row))
        k_dec_f = jnp.exp(lgf * (C - 1.0 - row))
        k_dec_b = jnp.exp(lgb * row)
        chunk_dec_f = jnp.exp(lgf * C)
        chunk_dec_b = jnp.exp(lgb * C)

        def masked_k(n):
            k = rk_ref[0, pl.ds(pl.multiple_of(n * C, C), C), :]
            return jnp.where(hmask, k, jnp.zeros_like(k)).astype(F32)

        def bwd_body(t, state):
            n = n_chunks - 1 - t
            sb_scr[hh, n] = state.astype(BF16)
            v = rv_ref[0, pl.ds(pl.multiple_of(n * C, C), C), vsl]
            return state * chunk_dec_b + _dot_tn((masked_k(n) * k_dec_b).astype(BF16), v)

        lax.fori_loop(0, n_chunks, bwd_body, jnp.zeros((LANES, RET_V_DIM), F32))

        def fwd_body(n, state):
            tok = pl.ds(pl.multiple_of(n * C, C), C)
            q = rq_ref[0, tok, :]
            v = rv_ref[0, tok, vsl]
            km = masked_k(n)
            s = _dot_nt(q, km.astype(BF16)) * decay
            qf = q.astype(F32)
            o = (_dot(s.astype(BF16), v)
                 + _dot((qf * q_dec_f).astype(BF16), state.astype(BF16))
                 + _dot((qf * q_dec_b).astype(BF16), sb_scr[hh, n]))
            o = _rms(o) * rg_ref[0, tok, vsl].astype(F32)
            o_ref[0, tok, vsl] = o.astype(BF16)
            return state * chunk_dec_f + _dot_tn((km * k_dec_f).astype(BF16), v)

        lax.fori_loop(0, n_chunks, fwd_body, jnp.zeros((LANES, RET_V_DIM), F32))


def _ret_stage(lgf, lgb, rq, rk, rv, rg):
    B, S, _ = rq.shape
    n_pairs = RET_HEADS // 2
    qk_spec = pl.BlockSpec((1, S, LANES), lambda b, p: (b, 0, p))
    v_spec = pl.BlockSpec((1, S, 2 * RET_V_DIM), lambda b, p: (b, 0, p))
    smem = pl.BlockSpec(memory_space=pltpu.SMEM)
    return pl.pallas_call(
        _ret_kernel,
        grid=(B, n_pairs),
        in_specs=[smem, smem, qk_spec, qk_spec, v_spec, v_spec],
        out_specs=v_spec,
        out_shape=jax.ShapeDtypeStruct((B, S, RET_HEADS * RET_V_DIM), BF16),
        scratch_shapes=[pltpu.VMEM((2, S // CHUNK, LANES, RET_V_DIM), BF16)],
        compiler_params=pltpu.CompilerParams(
            dimension_semantics=("parallel", "parallel"), vmem_limit_bytes=VMEM_LIMIT),
        name="ret_stage",
    )(lgf, lgb, rq, rk, rv, rg)


def _post_kernel(x_ref, at_ref, r_ref, sa_ref, sb_ref, wa_ref, wb_ref, wo_ref, gpm_ref,
                 gmlp_ref, wup_ref, wdn_ref, gpost_ref, y_ref):
    tm = x_ref.shape[1]
    a = _dot_tn(at_ref[0].reshape(MLA_HEADS * MLA_V_DIM, tm), wa_ref[...])
    b = _dot(r_ref[0], wb_ref[...])
    merged = sa_ref[0].astype(F32) * a + sb_ref[0].astype(F32) * b
    m = _dot(merged.astype(BF16), wo_ref[...])
    x1 = x_ref[0] + _rms(m, gpm_ref[...])
    h = _rms(x1, gmlp_ref[...]).astype(BF16)
    u = jnp.maximum(_dot(h, wup_ref[...]), 0.0)
    d = _dot((u * u).astype(BF16), wdn_ref[...])
    y_ref[0] = x1 + _rms(d, gpost_ref[...])


def _post_stage(x, at, r, sa, sb, w):
    B, S, D = x.shape
    tm = min(TM_POST, S)
    tok = lambda n: pl.BlockSpec((1, tm, n), lambda b, i: (b, i, 0))
    consts = [w["w_branch_a"], w["w_branch_b"], w["w_out"], w["g_post_mix"],
              w["g_pre_mlp"], w["w_up"], w["w_down"], w["g_post_mlp"]]
    return pl.pallas_call(
        _post_kernel,
        grid=(B, S // tm),
        in_specs=[tok(D), pl.BlockSpec((1, MLA_HEADS, MLA_V_DIM, tm), lambda b, i: (b, 0, 0, i)),
                  tok(RET_HEADS * RET_V_DIM), tok(D), tok(D)] + [_const_spec(c.shape) for c in consts],
        out_specs=tok(D),
        out_shape=jax.ShapeDtypeStruct((B, S, D), F32),
        compiler_params=pltpu.CompilerParams(
            dimension_semantics=("parallel", "parallel"), vmem_limit_bytes=VMEM_LIMIT),
        name="post_stage",
    )(x, at, r, sa, sb, *consts)


def _prepare_weights(g_pre_mix, w_in, g_q_norm, w_q_up, g_kv_norm, w_kv_up, w_branch_a, w_branch_b,
                     w_out, g_post_mix, g_pre_mlp, w_up, w_down, g_post_mlp):
    D = D_MODEL
    splits = (Q_LORA, KV_LORA, ROPE_DIM, RET_HEADS * RET_QK_DIM, RET_HEADS * RET_QK_DIM,
              RET_HEADS * RET_V_DIM, RET_HEADS * RET_V_DIM, D, D)
    offs = np.cumsum(splits)[:-1].tolist()
    w_cq, w_ckv, w_kr, w_rq, w_rk, w_rv, w_rg, w_ga, w_gb = jnp.split(w_in, offs, axis=-1)
    half = ROPE_DIM // 2
    zpad = jnp.zeros((D, LANES - ROPE_DIM), F32)
    w_kr_sw = jnp.concatenate([w_kr[:, half:], w_kr[:, :half]], axis=1)
    w_small = jnp.concatenate([w_cq, w_ckv, w_kr, zpad, w_kr_sw, zpad], axis=1)

    rh = RET_QK_DIM // 2
    perm = np.array([RET_QK_DIM * (2 * p + hh) + rh * part + r
                     for p in range(RET_HEADS // 2) for part in range(2) for hh in range(2) for r in range(rh)])

    wq = w_q_up.reshape(Q_LORA, MLA_HEADS, NOPE_DIM + ROPE_DIM)
    w_qn_t = wq[:, :, :NOPE_DIM].reshape(Q_LORA, MLA_HEADS * NOPE_DIM).T
    w_qr_t = wq[:, :, NOPE_DIM:].reshape(Q_LORA, MLA_HEADS * ROPE_DIM).T

    wkv = w_kv_up.reshape(KV_LORA, MLA_HEADS, NOPE_DIM + MLA_V_DIM)
    k_nope = jnp.pad(wkv[:, :, :NOPE_DIM], ((0, 0), (0, 0), (0, QK_PAD - NOPE_DIM)))
    rope_place = jnp.pad(jnp.eye(ROPE_DIM, dtype=F32), ((0, 0), (NOPE_DIM, QK_PAD - NOPE_DIM - ROPE_DIM)))
    rope_rows = jnp.broadcast_to(rope_place[:, None, :], (ROPE_DIM, MLA_HEADS, QK_PAD))
    w_k = jnp.concatenate(
        [k_nope, rope_rows, jnp.zeros((LANES - ROPE_DIM, MLA_HEADS, QK_PAD), F32)], axis=0
    ).reshape(KV_LORA + LANES, MLA_HEADS * QK_PAD)
    w_v_t = wkv[:, :, NOPE_DIM:].reshape(KV_LORA, MLA_HEADS * MLA_V_DIM).T

    bf = lambda a: a.astype(BF16)
    row = lambda g: g.reshape(1, -1)
    return dict(
        g_pre_mix=row(g_pre_mix), w_small=bf(w_small), w_rq=bf(w_rq[:, perm]), w_rk=bf(w_rk[:, perm]),
        w_rv=bf(w_rv), w_rg=bf(w_rg), w_ga=bf(w_ga), w_gb=bf(w_gb),
        g_q_norm=row(g_q_norm), w_qn_t=bf(w_qn_t), w_qr_t=bf(w_qr_t), g_kv_norm=row(g_kv_norm),
        w_k=bf(w_k), w_v_t=bf(w_v_t),
        w_branch_a=bf(w_branch_a), w_branch_b=bf(w_branch_b), w_out=bf(w_out), g_post_mix=row(g_post_mix),
        g_pre_mlp=row(g_pre_mlp), w_up=bf(w_up), w_down=bf(w_down), g_post_mlp=row(g_post_mlp),
    )


def _trunk(x, w, lgf, lgb):
    qt, k, vt, rq, rk, rv, rg, sa, sb = _pre_stage(x, w)
    at = _attn_stage(qt, k, vt)
    r = _ret_stage(lgf, lgb, rq, rk, rv, rg)
    return _post_stage(x, at, r, sa, sb, w)


def kernel(x_prompt, x_sample, g_pre_mix, w_in, g_q_norm, w_q_up, g_kv_norm, w_kv_up, w_branch_a,
           ret_log_decay_fwd, ret_log_decay_bwd, w_branch_b, w_out, g_post_mix, g_pre_mlp, w_up, w_down,
           g_post_mlp):
    assert w_in.shape[0] == 1, "single-layer trunk"
    w = _prepare_weights(g_pre_mix[0], w_in[0], g_q_norm[0], w_q_up[0], g_kv_norm[0], w_kv_up[0],
                         w_branch_a[0], w_branch_b[0], w_out[0], g_post_mix[0], g_pre_mlp[0],
                         w_up[0], w_down[0], g_post_mlp[0])
    lgf = ret_log_decay_fwd[0].astype(F32)
    lgb = ret_log_decay_bwd[0].astype(F32)
    return _trunk(x_prompt, w, lgf, lgb), _trunk(x_sample, w, lgf, lgb)
```

```python
import functools

import jax
import jax.numpy as jnp
import numpy as np
from jax import lax
from jax.experimental import pallas as pl
from jax.experimental.pallas import tpu as pltpu

F32 = jnp.float32
BF16 = jnp.bfloat16

D_MODEL = 1024
MLA_HEADS = 8
Q_LORA = 256
KV_LORA = 128
NOPE_DIM = 64
ROPE_DIM = 32
MLA_V_DIM = 64
RET_HEADS = 8
RET_QK_DIM = 64
RET_V_DIM = 128
CHUNK = 128
D_FF = 4 * D_MODEL
ROPE_THETA = 10000.0
EPS = 1e-6

LANES = 128
QK_PAD = LANES
SMALL_N = Q_LORA + KV_LORA + 2 * LANES
VMEM_LIMIT = 56 * 1024 * 1024

VT_ROWS = MLA_V_DIM + 16

TM_PRE = 512
TM_POST = 512
ATTN_TQ_SUB = 256
ATTN_TKV = 512
ATTN_MAX_CHUNKS = 16
ATTN_SKEW = 2
RET_UNROLL = 8


def _dot(a, b):
    return jnp.dot(a, b, preferred_element_type=F32)


def _dot_nt(a, b):
    return lax.dot_general(a, b, (((1,), (1,)), ((), ())), preferred_element_type=F32)


def _dot_tn(a, b):
    return lax.dot_general(a, b, (((0,), (0,)), ((), ())), preferred_element_type=F32)


def _rms(x, g=None):
    y = x * lax.rsqrt(jnp.mean(x * x, axis=-1, keepdims=True) + EPS)
    if g is not None:
        y = y * g
    return y


def _const_spec(shape):
    nd = len(shape)
    return pl.BlockSpec(shape, lambda *_: (0,) * nd, pipeline_mode=pl.Buffered(1))


def _pre_kernel(x_ref, g_ref, wsm_ref, wrq_ref, wrk_ref, wrv_ref, wrg_ref, wga_ref, wgb_ref,
                gq_ref, wqn_ref, wqr_ref, gkv_ref, wk_ref, wv_ref,
                cos_t_ref, sin_t_ref, cos_k_ref, sin_k_ref, cos_r_ref, sin_r_ref,
                qt_ref, k_ref, vt_ref, rq_ref, rk_ref, rv_ref, rg_ref, sa_ref, sb_ref):
    tm = x_ref.shape[1]
    h = _rms(x_ref[0], g_ref[...]).astype(BF16)

    sm = _dot(h, wsm_ref[...])
    cqn = _rms(sm[:, :Q_LORA], gq_ref[...]).astype(BF16)
    ckvn = _rms(sm[:, Q_LORA:Q_LORA + KV_LORA], gkv_ref[...]).astype(BF16)
    kr_a = sm[:, Q_LORA + KV_LORA:Q_LORA + KV_LORA + LANES]
    kr_b = sm[:, Q_LORA + KV_LORA + LANES:]
    k_rope = (kr_a * cos_k_ref[...] + kr_b * sin_k_ref[...]).astype(BF16)

    scale = (NOPE_DIM + ROPE_DIM) ** -0.5 * np.log2(np.e)
    qn = (_dot_nt(wqn_ref[...], cqn) * scale).reshape(MLA_HEADS, NOPE_DIM, tm)
    qr = (_dot_nt(wqr_ref[...], cqn) * scale).reshape(MLA_HEADS, ROPE_DIM, tm)
    half = ROPE_DIM // 2
    x1, x2 = qr[:, :half], qr[:, half:]
    cos_t, sin_t = cos_t_ref[...][None], sin_t_ref[...][None]
    qt = jnp.concatenate(
        [qn, x1 * cos_t - x2 * sin_t, x1 * sin_t + x2 * cos_t,
         jnp.zeros((MLA_HEADS, QK_PAD - NOPE_DIM - ROPE_DIM, tm), F32)], axis=1)
    qt_ref[0] = qt.astype(BF16)

    k_ref[0] = _dot(jnp.concatenate([ckvn, k_rope], axis=1), wk_ref[...]).astype(BF16)
    vt = _dot_nt(wv_ref[...], ckvn).reshape(MLA_HEADS, MLA_V_DIM, tm)
    ones = jnp.ones((MLA_HEADS, VT_ROWS - MLA_V_DIM, tm), F32)
    vt_ref[0] = jnp.concatenate([vt, ones], axis=1).astype(BF16)

    cos_r, sin_r = cos_r_ref[...], sin_r_ref[...]
    rq = _dot(h, wrq_ref[...])
    rk = _dot(h, wrk_ref[...]) * (RET_QK_DIM ** -0.5)
    for p in range(RET_HEADS // 2):
        sl = slice(p * LANES, (p + 1) * LANES)
        bq, bk = rq[:, sl], rk[:, sl]
        rq_ref[0, :, sl] = (bq * cos_r + pltpu.roll(bq, LANES // 2, 1) * sin_r).astype(BF16)
        rk_ref[0, :, sl] = (bk * cos_r + pltpu.roll(bk, LANES // 2, 1) * sin_r).astype(BF16)

    rv_ref[0] = _dot(h, wrv_ref[...]).astype(BF16)
    rg = _dot(h, wrg_ref[...])
    rg_ref[0] = (rg * jax.nn.sigmoid(rg)).astype(BF16)
    sa_ref[0] = jax.nn.sigmoid(_dot(h, wga_ref[...])).astype(BF16)
    sb_ref[0] = jax.nn.sigmoid(_dot(h, wgb_ref[...])).astype(BF16)


def _pre_stage(x, w):
    B, S, D = x.shape
    tm = min(TM_PRE, S)
    grid = (B, S // tm)
    tok = lambda n: pl.BlockSpec((1, tm, n), lambda b, i: (b, i, 0))
    tab = lambda n: pl.BlockSpec((tm, n), lambda b, i: (i, 0))
    tab_t = pl.BlockSpec((ROPE_DIM // 2, tm), lambda b, i: (0, i))
    head_t = lambda n: pl.BlockSpec((1, MLA_HEADS, n, tm), lambda b, i: (b, 0, 0, i))
    consts = [w["g_pre_mix"], w["w_small"], w["w_rq"], w["w_rk"], w["w_rv"], w["w_rg"], w["w_ga"], w["w_gb"],
              w["g_q_norm"], w["w_qn_t"], w["w_qr_t"], w["g_kv_norm"], w["w_k"], w["w_v_t"]]
    cos_t, sin_t, cos_k, sin_k, cos_r, sin_r = _rope_tables(S)
    out_shape = [
        jax.ShapeDtypeStruct((B, MLA_HEADS, QK_PAD, S), BF16),
        jax.ShapeDtypeStruct((B, S, MLA_HEADS * QK_PAD), BF16),
        jax.ShapeDtypeStruct((B, MLA_HEADS, VT_ROWS, S), BF16),
        jax.ShapeDtypeStruct((B, S, RET_HEADS * RET_QK_DIM), BF16),
        jax.ShapeDtypeStruct((B, S, RET_HEADS * RET_QK_DIM), BF16),
        jax.ShapeDtypeStruct((B, S, RET_HEADS * RET_V_DIM), BF16),
        jax.ShapeDtypeStruct((B, S, RET_HEADS * RET_V_DIM), BF16),
        jax.ShapeDtypeStruct((B, S, D), BF16),
        jax.ShapeDtypeStruct((B, S, D), BF16),
    ]
    out_specs = [head_t(QK_PAD), tok(MLA_HEADS * QK_PAD), head_t(VT_ROWS),
                 tok(RET_HEADS * RET_QK_DIM), tok(RET_HEADS * RET_QK_DIM),
                 tok(RET_HEADS * RET_V_DIM), tok(RET_HEADS * RET_V_DIM), tok(D), tok(D)]
    return pl.pallas_call(
        _pre_kernel,
        grid=grid,
        in_specs=[tok(D)] + [_const_spec(c.shape) for c in consts]
                 + [tab_t, tab_t, tab(LANES), tab(LANES), tab(LANES), tab(LANES)],
        out_specs=out_specs,
        out_shape=out_shape,
        compiler_params=pltpu.CompilerParams(
            dimension_semantics=("parallel", "parallel"), vmem_limit_bytes=VMEM_LIMIT),
        name="pre_stage",
    )(x, *consts, cos_t, sin_t, cos_k, sin_k, cos_r, sin_r)


def _rope_angles(seq, dim):
    inv = 1.0 / (ROPE_THETA ** (jnp.arange(0, dim, 2, dtype=F32) / dim))
    ang = jnp.arange(seq, dtype=F32)[:, None] * inv[None, :]
    return jnp.cos(ang), jnp.sin(ang)


def _rope_tables(S):
    cos_m, sin_m = _rope_angles(S, ROPE_DIM)
    zpad = jnp.zeros((S, LANES - ROPE_DIM), F32)
    cos_k = jnp.concatenate([cos_m, cos_m, zpad], axis=1)
    sin_k = jnp.concatenate([-sin_m, sin_m, zpad], axis=1)
    cos_r, sin_r = _rope_angles(S, RET_QK_DIM)
    cos_r4 = jnp.concatenate([cos_r] * 4, axis=1)
    sin_r4 = jnp.concatenate([-sin_r, -sin_r, sin_r, sin_r], axis=1)
    return cos_m.T, sin_m.T, cos_k, sin_k, cos_r4, sin_r4


def _attn_kernel(qt_ref, k_ref, vt_ref, o_ref, s0_scr, s1_scr, *, tkv, tqs, skew):
    S = k_ref.shape[1]
    n_kv = S // tkv
    n_q = S // tqs
    assert n_q % 2 == 0 and n_kv >= skew
    sub = 8
    mx_init = jnp.full((sub, tqs), -jnp.inf, F32)
    acc_init = jnp.zeros((VT_ROWS, tqs), F32)

    def q_cols(j):
        return pl.ds(pl.multiple_of(j * tqs, tqs), tqs)

    def kv_rows(c):
        return pl.ds(c * tkv, tkv)

    def pass_a(c, qt, dst, mx):
        s = _dot(k_ref[0, kv_rows(c), :], qt)
        dst[kv_rows(c), :] = s
        return jnp.maximum(mx, jnp.max(s.reshape(tkv // sub, sub, tqs), axis=0))

    def pass_b(c, m, src, acc):
        p = jnp.exp2(src[kv_rows(c), :] - m).astype(BF16)
        return acc + _dot(vt_ref[0, 0, :, kv_rows(c)], p)

    def col_max(mx):
        return jnp.max(mx, axis=0, keepdims=True)

    def finish(j, acc):
        o_ref[0, 0, :, q_cols(j)] = (acc[:MLA_V_DIM] / acc[MLA_V_DIM:MLA_V_DIM + 1]).astype(BF16)

    def scores_only(j, dst):
        qt = qt_ref[0, 0, :, q_cols(j)]
        mx = mx_init
        for c in range(n_kv):
            mx = pass_a(c, qt, dst, mx)
        return col_max(mx)

    def softmax_and_next_scores(j, m, src, dst):
        qt_next = qt_ref[0, 0, :, q_cols(j + 1)]
        acc, mx = acc_init, mx_init
        for c in range(n_kv + skew):
            if c < n_kv:
                mx = pass_a(c, qt_next, dst, mx)
            if c >= skew:
                acc = pass_b(c - skew, m, src, acc)
        finish(j, acc)
        return col_max(mx)

    def softmax_only(j, m, src):
        acc = acc_init
        for c in range(n_kv):
            acc = pass_b(c, m, src, acc)
        finish(j, acc)

    def two_tiles(jj, m):
        m = softmax_and_next_scores(2 * jj, m, s0_scr, s1_scr)
        return softmax_and_next_scores(2 * jj + 1, m, s1_scr, s0_scr)

    m = lax.fori_loop(0, n_q // 2 - 1, two_tiles, scores_only(0, s0_scr))
    m = softmax_and_next_scores(n_q - 2, m, s0_scr, s1_scr)
    softmax_only(n_q - 1, m, s1_scr)


def _attn_stage(qt, k, vt):
    B, H, _, S = qt.shape
    tqs = min(ATTN_TQ_SUB, S)
    tkv = max(min(ATTN_TKV, S), S // ATTN_MAX_CHUNKS)
    return pl.pallas_call(
        functools.partial(_attn_kernel, tkv=tkv, tqs=tqs, skew=ATTN_SKEW),
        grid=(B, H),
        in_specs=[
            pl.BlockSpec((1, 1, QK_PAD, S), lambda b, h: (b, h, 0, 0)),
            pl.BlockSpec((1, S, QK_PAD), lambda b, h: (b, 0, h)),
            pl.BlockSpec((1, 1, VT_ROWS, S), lambda b, h: (b, h, 0, 0)),
        ],
        out_specs=pl.BlockSpec((1, 1, MLA_V_DIM, S), lambda b, h: (b, h, 0, 0)),
        out_shape=jax.ShapeDtypeStruct((B, H, MLA_V_DIM, S), BF16),
        scratch_shapes=[pltpu.VMEM((S, tqs), F32), pltpu.VMEM((S, tqs), F32)],
        compiler_params=pltpu.CompilerParams(
            dimension_semantics=("parallel", "parallel"), vmem_limit_bytes=VMEM_LIMIT),
        name="attn_stage",
    )(qt, k, vt)


def _ret_kernel(lgf_ref, lgb_ref, rq_ref, rk_ref, rv_ref, rg_ref, o_ref, sf_scr, sb_scr, *, unroll):
    C, Dv = CHUNK, RET_V_DIM
    W = 2 * Dv
    n_chunks = rq_ref.shape[1] // C
    pair = pl.program_id(1)

    def iota(shape, dim):
        return lax.broadcasted_iota(jnp.int32, shape, dim)

    def rate(head_idx, ref):
        return jnp.where(head_idx == 0, ref[2 * pair], ref[2 * pair + 1])

    lane_head = (iota((1, LANES), 1) // (RET_QK_DIM // 2)) % 2
    col_head = iota((1, W), 1) // Dv
    lgf_lane, lgb_lane = rate(lane_head, lgf_ref), rate(lane_head, lgb_ref)
    lgf_col, lgb_col = rate(col_head, lgf_ref), rate(col_head, lgb_ref)
    row = iota((C, LANES), 0).astype(F32)
    q_dec_f = jnp.exp(lgf_lane * (row + 1.0))
    q_dec_b = jnp.exp(lgb_lane * (C - row))
    k_dec_f = jnp.exp(lgf_lane * (C - 1.0 - row))
    k_dec_b = jnp.exp(lgb_lane * row)
    diff = (iota((C, W), 0) - iota((C, W), 1) % C).astype(F32)
    decay = jnp.where(diff >= 0, jnp.exp(lgf_col * jnp.maximum(diff, 0.0)),
                      jnp.exp(lgb_col * jnp.maximum(-diff, 0.0)))
    chunk_dec_f = jnp.exp(lgf_col * C)
    chunk_dec_b = jnp.exp(lgb_col * C)
    own = (iota((LANES, W), 0) // (RET_QK_DIM // 2)) % 2 == iota((LANES, W), 1) // Dv
    h0_lane = lane_head == 0
    h0_col = col_head == 0

    def chunk(n):
        return pl.ds(pl.multiple_of(n * C, C), C)

    def kv(n, k_dec):
        return _dot_tn((rk_ref[0, chunk(n), :].astype(F32) * k_dec).astype(BF16), rv_ref[0, chunk(n), :])

    def state_body(i, carry):
        sf, sb = carry
        fwd = [i * unroll + u for u in range(unroll)]
        bwd = [n_chunks - 1 - t for t in fwd]
        kv_f = [kv(t, k_dec_f) for t in fwd]
        kv_b = [kv(n, k_dec_b) for n in bwd]
        for u in range(unroll):
            sf_scr[fwd[u]] = sf.astype(BF16)
            sb_scr[bwd[u]] = sb.astype(BF16)
            sf = sf * chunk_dec_f + jnp.where(own, kv_f[u], 0.0)
            sb = sb * chunk_dec_b + jnp.where(own, kv_b[u], 0.0)
        return sf, sb

    zero = jnp.zeros((LANES, W), F32)
    lax.fori_loop(0, n_chunks // unroll, state_body, (zero, zero))

    def scores(n):
        k = rk_ref[0, chunk(n), :]
        k_heads = jnp.concatenate([jnp.where(h0_lane, k, jnp.zeros_like(k)),
                                   jnp.where(h0_lane, jnp.zeros_like(k), k)], axis=0)
        return _dot_nt(rq_ref[0, chunk(n), :], k_heads)

    def outputs(n, s):
        qf = rq_ref[0, chunk(n), :].astype(F32)
        v = rv_ref[0, chunk(n), :]
        lhs = jnp.concatenate([(s * decay).astype(BF16), (qf * q_dec_f).astype(BF16),
                               (qf * q_dec_b).astype(BF16)], axis=1)
        v_heads = jnp.concatenate([jnp.where(h0_col, v, jnp.zeros_like(v)),
                                   jnp.where(h0_col, jnp.zeros_like(v), v)], axis=0)
        return _dot(lhs, jnp.concatenate([v_heads, sf_scr[n], sb_scr[n]], axis=0))

    def out_body(i, carry):
        chunks = [i * unroll + u for u in range(unroll)]
        s = [scores(n) for n in chunks]
        o = [outputs(n, s_n) for n, s_n in zip(chunks, s)]
        for n, o_n in zip(chunks, o):
            normed = jnp.concatenate([_rms(o_n[:, :Dv]), _rms(o_n[:, Dv:])], axis=1)
            o_ref[0, chunk(n), :] = (normed * rg_ref[0, chunk(n), :].astype(F32)).astype(BF16)
        return carry

    lax.fori_loop(0, n_chunks // unroll, out_body, 0)


def _ret_stage(lgf, lgb, rq, rk, rv, rg):
    B, S, _ = rq.shape
    qk_spec = pl.BlockSpec((1, S, LANES), lambda b, p: (b, 0, p))
    v_spec = pl.BlockSpec((1, S, 2 * RET_V_DIM), lambda b, p: (b, 0, p))
    smem = pl.BlockSpec(memory_space=pltpu.SMEM)
    state_scr = pltpu.VMEM((S // CHUNK, LANES, 2 * RET_V_DIM), BF16)
    assert (S // CHUNK) % RET_UNROLL == 0
    return pl.pallas_call(
        functools.partial(_ret_kernel, unroll=RET_UNROLL),
        grid=(B, RET_HEADS // 2),
        in_specs=[smem, smem, qk_spec, qk_spec, v_spec, v_spec],
        out_specs=v_spec,
        out_shape=jax.ShapeDtypeStruct((B, S, RET_HEADS * RET_V_DIM), BF16),
        scratch_shapes=[state_scr, state_scr],
        compiler_params=pltpu.CompilerParams(
            dimension_semantics=("parallel", "parallel"), vmem_limit_bytes=VMEM_LIMIT),
        name="ret_stage",
    )(lgf, lgb, rq, rk, rv, rg)


def _post_kernel(x_ref, at_ref, r_ref, sa_ref, sb_ref, wa_ref, wb_ref, wo_ref, gpm_ref,
                 gmlp_ref, wup_ref, wdn_ref, gpost_ref, y_ref):
    tm = x_ref.shape[1]
    a = _dot_tn(at_ref[0].reshape(MLA_HEADS * MLA_V_DIM, tm), wa_ref[...])
    b = _dot(r_ref[0], wb_ref[...])
    merged = sa_ref[0].astype(F32) * a + sb_ref[0].astype(F32) * b
    m = _dot(merged.astype(BF16), wo_ref[...])
    x1 = x_ref[0] + _rms(m, gpm_ref[...])
    h = _rms(x1, gmlp_ref[...]).astype(BF16)
    u = jnp.maximum(_dot(h, wup_ref[...]), 0.0)
    d = _dot((u * u).astype(BF16), wdn_ref[...])
    y_ref[0] = x1 + _rms(d, gpost_ref[...])


def _post_stage(x, at, r, sa, sb, w):
    B, S, D = x.shape
    tm = min(TM_POST, S)
    tok = lambda n: pl.BlockSpec((1, tm, n), lambda b, i: (b, i, 0))
    consts = [w["w_branch_a"], w["w_branch_b"], w["w_out"], w["g_post_mix"],
              w["g_pre_mlp"], w["w_up"], w["w_down"], w["g_post_mlp"]]
    return pl.pallas_call(
        _post_kernel,
        grid=(B, S // tm),
        in_specs=[tok(D), pl.BlockSpec((1, MLA_HEADS, MLA_V_DIM, tm), lambda b, i: (b, 0, 0, i)),
                  tok(RET_HEADS * RET_V_DIM), tok(D), tok(D)] + [_const_spec(c.shape) for c in consts],
        out_specs=tok(D),
        out_shape=jax.ShapeDtypeStruct((B, S, D), F32),
        compiler_params=pltpu.CompilerParams(
            dimension_semantics=("parallel", "parallel"), vmem_limit_bytes=VMEM_LIMIT),
        name="post_stage",
    )(x, at, r, sa, sb, *consts)


def _prepare_weights(g_pre_mix, w_in, g_q_norm, w_q_up, g_kv_norm, w_kv_up, w_branch_a, w_branch_b,
                     w_out, g_post_mix, g_pre_mlp, w_up, w_down, g_post_mlp):
    D = D_MODEL
    splits = (Q_LORA, KV_LORA, ROPE_DIM, RET_HEADS * RET_QK_DIM, RET_HEADS * RET_QK_DIM,
              RET_HEADS * RET_V_DIM, RET_HEADS * RET_V_DIM, D, D)
    offs = np.cumsum(splits)[:-1].tolist()
    w_cq, w_ckv, w_kr, w_rq, w_rk, w_rv, w_rg, w_ga, w_gb = jnp.split(w_in, offs, axis=-1)
    half = ROPE_DIM // 2
    zpad = jnp.zeros((D, LANES - ROPE_DIM), F32)
    w_kr_sw = jnp.concatenate([w_kr[:, half:], w_kr[:, :half]], axis=1)
    w_small = jnp.concatenate([w_cq, w_ckv, w_kr, zpad, w_kr_sw, zpad], axis=1)

    rh = RET_QK_DIM // 2
    perm = np.array([RET_QK_DIM * (2 * p + hh) + rh * part + r
                     for p in range(RET_HEADS // 2) for part in range(2) for hh in range(2) for r in range(rh)])

    wq = w_q_up.reshape(Q_LORA, MLA_HEADS, NOPE_DIM + ROPE_DIM)
    w_qn_t = wq[:, :, :NOPE_DIM].reshape(Q_LORA, MLA_HEADS * NOPE_DIM).T
    w_qr_t = wq[:, :, NOPE_DIM:].reshape(Q_LORA, MLA_HEADS * ROPE_DIM).T

    wkv = w_kv_up.reshape(KV_LORA, MLA_HEADS, NOPE_DIM + MLA_V_DIM)
    k_nope = jnp.pad(wkv[:, :, :NOPE_DIM], ((0, 0), (0, 0), (0, QK_PAD - NOPE_DIM)))
    rope_place = jnp.pad(jnp.eye(ROPE_DIM, dtype=F32), ((0, 0), (NOPE_DIM, QK_PAD - NOPE_DIM - ROPE_DIM)))
    rope_rows = jnp.broadcast_to(rope_place[:, None, :], (ROPE_DIM, MLA_HEADS, QK_PAD))
    w_k = jnp.concatenate(
        [k_nope, rope_rows, jnp.zeros((LANES - ROPE_DIM, MLA_HEADS, QK_PAD), F32)], axis=0
    ).reshape(KV_LORA + LANES, MLA_HEADS * QK_PAD)
    w_v_t = wkv[:, :, NOPE_DIM:].reshape(KV_LORA, MLA_HEADS * MLA_V_DIM).T

    bf = lambda a: a.astype(BF16)
    row = lambda g: g.reshape(1, -1)
    return dict(
        g_pre_mix=row(g_pre_mix), w_small=bf(w_small), w_rq=bf(w_rq[:, perm]), w_rk=bf(w_rk[:, perm]),
        w_rv=bf(w_rv), w_rg=bf(w_rg), w_ga=bf(w_ga), w_gb=bf(w_gb),
        g_q_norm=row(g_q_norm), w_qn_t=bf(w_qn_t), w_qr_t=bf(w_qr_t), g_kv_norm=row(g_kv_norm),
        w_k=bf(w_k), w_v_t=bf(w_v_t),
        w_branch_a=bf(w_branch_a), w_branch_b=bf(w_branch_b), w_out=bf(w_out), g_post_mix=row(g_post_mix),
        g_pre_mlp=row(g_pre_mlp), w_up=bf(w_up), w_down=bf(w_down), g_post_mlp=row(g_post_mlp),
    )


def _trunk(x, w, lgf, lgb):
    qt, k, vt, rq, rk, rv, rg, sa, sb = _pre_stage(x, w)
    at = _attn_stage(qt, k, vt)
    r = _ret_stage(lgf, lgb, rq, rk, rv, rg)
    return _post_stage(x, at, r, sa, sb, w)


def kernel(x_prompt, x_sample, g_pre_mix, w_in, g_q_norm, w_q_up, g_kv_norm, w_kv_up, w_branch_a,
           ret_log_decay_fwd, ret_log_decay_bwd, w_branch_b, w_out, g_post_mix, g_pre_mlp, w_up, w_down,
           g_post_mlp):
    assert w_in.shape[0] == 1, "single-layer trunk"
    w = _prepare_weights(g_pre_mix[0], w_in[0], g_q_norm[0], w_q_up[0], g_kv_norm[0], w_kv_up[0],
                         w_branch_a[0], w_branch_b[0], w_out[0], g_post_mix[0], g_pre_mlp[0],
                         w_up[0], w_down[0], g_post_mlp[0])
    lgf = ret_log_decay_fwd[0].astype(F32)
    lgb = ret_log_decay_bwd[0].astype(F32)
    return _trunk(x_prompt, w, lgf, lgb), _trunk(x_sample, w, lgf, lgb)
```

```python
import functools

import jax
import jax.numpy as jnp
import numpy as np
from jax import lax
from jax.experimental import pallas as pl
from jax.experimental.pallas import tpu as pltpu

F32 = jnp.float32
BF16 = jnp.bfloat16

D_MODEL = 1024
MLA_HEADS = 8
Q_LORA = 256
KV_LORA = 128
NOPE_DIM = 64
ROPE_DIM = 32
MLA_V_DIM = 64
RET_HEADS = 8
RET_QK_DIM = 64
RET_V_DIM = 128
CHUNK = 128
D_FF = 4 * D_MODEL
ROPE_THETA = 10000.0
EPS = 1e-6

LANES = 128
QK_PAD = LANES
SMALL_N = Q_LORA + KV_LORA + LANES
VMEM_LIMIT = 56 * 1024 * 1024

VT_ROWS = MLA_V_DIM + 16

TM_PRE = 512
PRE_SPLIT = 2
TM_POST = 512
POST_SPLIT = 2
ATTN_TQ = 256
ATTN_TKV = 512
ATTN_MAX_CHUNKS = 16
ATTN_STEP_KEYS = 8192
ATTN_SKEW = 2
RET_UNROLL = 8


def _dot(a, b):
    return jnp.dot(a, b, preferred_element_type=F32)


def _dot_nt(a, b):
    return lax.dot_general(a, b, (((1,), (1,)), ((), ())), preferred_element_type=F32)


def _dot_tn(a, b):
    return lax.dot_general(a, b, (((0,), (0,)), ((), ())), preferred_element_type=F32)


def _rms(x, g=None):
    y = x * lax.rsqrt(jnp.mean(x * x, axis=-1, keepdims=True) + EPS)
    if g is not None:
        y = y * g
    return y


def _const_spec(shape):
    nd = len(shape)
    return pl.BlockSpec(shape, lambda *_: (0,) * nd, pipeline_mode=pl.Buffered(1))


def _pre_kernel(x_ref, g_ref, wsm_ref, wrq_ref, wrk_ref, wrv_ref, wrg_ref, wga_ref, wgb_ref,
                gq_ref, wqn_ref, wqr_ref, gkv_ref, wk_ref, wv_ref,
                cos_t_ref, sin_t_ref, cos_k_ref, sin_k_ref, cos_r_ref, sin_r_ref,
                qt_ref, k_ref, vt_ref, rq_ref, rk_ref, rv_ref, rg_ref, sa_ref, sb_ref):
    tm = x_ref.shape[1]
    ts = tm // PRE_SPLIT
    parts = [slice(i * ts, (i + 1) * ts) for i in range(PRE_SPLIT)]
    each = lambda fn, *lists: [fn(*args) for args in zip(*lists)]

    h = [_rms(x_ref[0, r, :], g_ref[...]).astype(BF16) for r in parts]
    sm = each(lambda a: _dot(a, wsm_ref[...]), h)
    rq = each(lambda a: _dot(a, wrq_ref[...]), h)
    rk = each(lambda a: _dot(a, wrk_ref[...]) * (RET_QK_DIM ** -0.5), h)
    rv = each(lambda a: _dot(a, wrv_ref[...]), h)

    cqn = each(lambda a: _rms(a[:, :Q_LORA], gq_ref[...]).astype(BF16), sm)
    ckvn = each(lambda a: _rms(a[:, Q_LORA:Q_LORA + KV_LORA], gkv_ref[...]).astype(BF16), sm)
    kr = [a[:, Q_LORA + KV_LORA:] for a in sm]
    k_rope = [(t * cos_k_ref[r, :] + pltpu.roll(t, LANES - ROPE_DIM, 1) * sin_k_ref[r, :]).astype(BF16)
              for t, r in zip(kr, parts)]

    scale = (NOPE_DIM + ROPE_DIM) ** -0.5 * np.log2(np.e)
    qn = each(lambda a: _dot_nt(wqn_ref[...], a) * scale, cqn)
    qr = each(lambda a: _dot_nt(wqr_ref[...], a) * scale, cqn)
    kk = each(lambda a, b: _dot(jnp.concatenate([a, b], axis=1), wk_ref[...]), ckvn, k_rope)
    vt = each(lambda a: _dot_nt(wv_ref[...], a), ckvn)
    rg = each(lambda a: _dot(a, wrg_ref[...]), h)
    ga = each(lambda a: _dot(a, wga_ref[...]), h)
    gb = each(lambda a: _dot(a, wgb_ref[...]), h)

    half = ROPE_DIM // 2
    cos_r, sin_r = cos_r_ref[...], sin_r_ref[...]
    for i, r in enumerate(parts):
        for p in range(RET_HEADS // 2):
            sl = slice(p * LANES, (p + 1) * LANES)
            bq, bk = rq[i][:, sl], rk[i][:, sl]
            rq_ref[0, r, sl] = (bq * cos_r[r] + pltpu.roll(bq, LANES // 2, 1) * sin_r[r]).astype(BF16)
            rk_ref[0, r, sl] = (bk * cos_r[r] + pltpu.roll(bk, LANES // 2, 1) * sin_r[r]).astype(BF16)
        rv_ref[0, r, :] = rv[i].astype(BF16)

        q_rope = qr[i].reshape(MLA_HEADS, ROPE_DIM, ts)
        x1, x2 = q_rope[:, :half], q_rope[:, half:]
        cos_t, sin_t = cos_t_ref[:, r][None], sin_t_ref[:, r][None]
        qt = jnp.concatenate(
            [qn[i].reshape(MLA_HEADS, NOPE_DIM, ts), x1 * cos_t - x2 * sin_t, x1 * sin_t + x2 * cos_t,
             jnp.zeros((MLA_HEADS, QK_PAD - NOPE_DIM - ROPE_DIM, ts), F32)], axis=1)
        qt_ref[0, :, :, r] = qt.astype(BF16)
        k_ref[0, r, :] = kk[i].astype(BF16)
        ones = jnp.ones((MLA_HEADS, VT_ROWS - MLA_V_DIM, ts), F32)
        vt_ref[0, :, :, r] = jnp.concatenate([vt[i].reshape(MLA_HEADS, MLA_V_DIM, ts), ones], axis=1).astype(BF16)

        rg_ref[0, r, :] = (rg[i] * jax.nn.sigmoid(rg[i])).astype(BF16)
        sa_ref[0, r, :] = jax.nn.sigmoid(ga[i]).astype(BF16)
        sb_ref[0, r, :] = jax.nn.sigmoid(gb[i]).astype(BF16)


def _pre_stage(x, w):
    B, S, D = x.shape
    tm = min(TM_PRE, S)
    grid = (B, S // tm)
    tok = lambda n: pl.BlockSpec((1, tm, n), lambda b, i: (b, i, 0))
    tab = lambda n: pl.BlockSpec((tm, n), lambda b, i: (i, 0))
    tab_t = pl.BlockSpec((ROPE_DIM // 2, tm), lambda b, i: (0, i))
    head_t = lambda n: pl.BlockSpec((1, MLA_HEADS, n, tm), lambda b, i: (b, 0, 0, i))
    consts = [w["g_pre_mix"], w["w_small"], w["w_rq"], w["w_rk"], w["w_rv"], w["w_rg"], w["w_ga"], w["w_gb"],
              w["g_q_norm"], w["w_qn_t"], w["w_qr_t"], w["g_kv_norm"], w["w_k"], w["w_v_t"]]
    cos_t, sin_t, cos_k, sin_k, cos_r, sin_r = _rope_tables(S)
    out_shape = [
        jax.ShapeDtypeStruct((B, MLA_HEADS, QK_PAD, S), BF16),
        jax.ShapeDtypeStruct((B, S, MLA_HEADS * QK_PAD), BF16),
        jax.ShapeDtypeStruct((B, MLA_HEADS, VT_ROWS, S), BF16),
        jax.ShapeDtypeStruct((B, S, RET_HEADS * RET_QK_DIM), BF16),
        jax.ShapeDtypeStruct((B, S, RET_HEADS * RET_QK_DIM), BF16),
        jax.ShapeDtypeStruct((B, S, RET_HEADS * RET_V_DIM), BF16),
        jax.ShapeDtypeStruct((B, S, RET_HEADS * RET_V_DIM), BF16),
        jax.ShapeDtypeStruct((B, S, D), BF16),
        jax.ShapeDtypeStruct((B, S, D), BF16),
    ]
    out_specs = [head_t(QK_PAD), tok(MLA_HEADS * QK_PAD), head_t(VT_ROWS),
                 tok(RET_HEADS * RET_QK_DIM), tok(RET_HEADS * RET_QK_DIM),
                 tok(RET_HEADS * RET_V_DIM), tok(RET_HEADS * RET_V_DIM), tok(D), tok(D)]
    return pl.pallas_call(
        _pre_kernel,
        grid=grid,
        in_specs=[tok(D)] + [_const_spec(c.shape) for c in consts]
                 + [tab_t, tab_t, tab(LANES), tab(LANES), tab(LANES), tab(LANES)],
        out_specs=out_specs,
        out_shape=out_shape,
        compiler_params=pltpu.CompilerParams(
            dimension_semantics=("parallel", "parallel"), vmem_limit_bytes=VMEM_LIMIT),
        name="pre_stage",
    )(x, *consts, cos_t, sin_t, cos_k, sin_k, cos_r, sin_r)


def _rope_angles(seq, dim):
    inv = 1.0 / (ROPE_THETA ** (jnp.arange(0, dim, 2, dtype=F32) / dim))
    ang = jnp.arange(seq, dtype=F32)[:, None] * inv[None, :]
    return jnp.cos(ang), jnp.sin(ang)


def _rope_tables(S):
    cos_m, sin_m = _rope_angles(S, ROPE_DIM)
    zpad = jnp.zeros((S, LANES - ROPE_DIM), F32)
    cos_k = jnp.concatenate([cos_m, cos_m, zpad], axis=1)
    sin_k = jnp.concatenate([-sin_m, sin_m, zpad], axis=1)
    cos_r, sin_r = _rope_angles(S, RET_QK_DIM)
    cos_r4 = jnp.concatenate([cos_r] * 4, axis=1)
    sin_r4 = jnp.concatenate([-sin_r, -sin_r, sin_r, sin_r], axis=1)
    return cos_m.T, sin_m.T, cos_k, sin_k, cos_r4, sin_r4


def _attn_kernel(qt_ref, k_ref, vt_ref, o_ref, s0_scr, s1_scr, *, tkv, tqs, skew):
    S = k_ref.shape[1]
    n_kv = S // tkv
    n_q = S // tqs
    n_tiles = qt_ref.shape[1] * n_q
    assert n_tiles % 2 == 0 and n_kv >= skew
    sub = 8
    mx_init = jnp.full((sub, tqs), -jnp.inf, F32)
    acc_init = jnp.zeros((VT_ROWS, tqs), F32)

    def head_of(t):
        return t // n_q

    def q_cols(t):
        return pl.ds(pl.multiple_of((t % n_q) * tqs, tqs), tqs)

    def kv_rows(c):
        return pl.ds(c * tkv, tkv)

    def pass_a(c, t, qt, dst, mx):
        k = k_ref[0, kv_rows(c), pl.ds(pl.multiple_of(head_of(t) * QK_PAD, QK_PAD), QK_PAD)]
        s = _dot(k, qt)
        dst[kv_rows(c), :] = s
        return jnp.maximum(mx, jnp.max(s.reshape(tkv // sub, sub, tqs), axis=0))

    def pass_b(c, t, m, src, acc):
        p = jnp.exp2(src[kv_rows(c), :] - m).astype(BF16)
        return acc + _dot(vt_ref[0, head_of(t), :, kv_rows(c)], p)

    def col_max(mx):
        return jnp.max(mx, axis=0, keepdims=True)

    def finish(t, acc):
        o_ref[0, head_of(t), :, q_cols(t)] = (acc[:MLA_V_DIM] / acc[MLA_V_DIM:MLA_V_DIM + 1]).astype(BF16)

    def scores_only(t, dst):
        qt = qt_ref[0, head_of(t), :, q_cols(t)]
        mx = mx_init
        for c in range(n_kv):
            mx = pass_a(c, t, qt, dst, mx)
        return col_max(mx)

    def softmax_and_next_scores(t, m, src, dst):
        qt_next = qt_ref[0, head_of(t + 1), :, q_cols(t + 1)]
        acc, mx = acc_init, mx_init
        for c in range(n_kv + skew):
            if c < n_kv:
                mx = pass_a(c, t + 1, qt_next, dst, mx)
            if c >= skew:
                acc = pass_b(c - skew, t, m, src, acc)
        finish(t, acc)
        return col_max(mx)

    def softmax_only(t, m, src):
        acc = acc_init
        for c in range(n_kv):
            acc = pass_b(c, t, m, src, acc)
        finish(t, acc)

    def two_tiles(i, m):
        m = softmax_and_next_scores(2 * i, m, s0_scr, s1_scr)
        return softmax_and_next_scores(2 * i + 1, m, s1_scr, s0_scr)

    m = lax.fori_loop(0, n_tiles // 2 - 1, two_tiles, scores_only(0, s0_scr))
    m = softmax_and_next_scores(n_tiles - 2, m, s0_scr, s1_scr)
    softmax_only(n_tiles - 1, m, s1_scr)


def _attn_stage(qt, k, vt):
    B, H, _, S = qt.shape
    tqs = min(ATTN_TQ, S)
    tkv = max(min(ATTN_TKV, S), S // ATTN_MAX_CHUNKS)
    g = max(1, min(H, ATTN_STEP_KEYS // S))
    return pl.pallas_call(
        functools.partial(_attn_kernel, tkv=tkv, tqs=tqs, skew=ATTN_SKEW),
        grid=(B, H // g),
        in_specs=[
            pl.BlockSpec((1, g, QK_PAD, S), lambda b, h: (b, h, 0, 0)),
            pl.BlockSpec((1, S, g * QK_PAD), lambda b, h: (b, 0, h)),
            pl.BlockSpec((1, g, VT_ROWS, S), lambda b, h: (b, h, 0, 0)),
        ],
        out_specs=pl.BlockSpec((1, g, MLA_V_DIM, S), lambda b, h: (b, h, 0, 0)),
        out_shape=jax.ShapeDtypeStruct((B, H, MLA_V_DIM, S), BF16),
        scratch_shapes=[pltpu.VMEM((S, tqs), F32), pltpu.VMEM((S, tqs), F32)],
        compiler_params=pltpu.CompilerParams(
            dimension_semantics=("parallel", "parallel"), vmem_limit_bytes=VMEM_LIMIT),
        name="attn_stage",
    )(qt, k, vt)


def _ret_kernel(lgf_ref, lgb_ref, rq_ref, rk_ref, rv_ref, rg_ref, o_ref, sf_scr, sb_scr, *, unroll):
    C, Dv = CHUNK, RET_V_DIM
    W = 2 * Dv
    n_chunks = rq_ref.shape[1] // C
    pair = pl.program_id(1)

    def iota(shape, dim):
        return lax.broadcasted_iota(jnp.int32, shape, dim)

    def rate(head_idx, ref):
        return jnp.where(head_idx == 0, ref[2 * pair], ref[2 * pair + 1])

    lane_head = (iota((1, LANES), 1) // (RET_QK_DIM // 2)) % 2
    col_head = iota((1, W), 1) // Dv
    lgf_lane, lgb_lane = rate(lane_head, lgf_ref), rate(lane_head, lgb_ref)
    lgf_col, lgb_col = rate(col_head, lgf_ref), rate(col_head, lgb_ref)
    row = iota((C, LANES), 0).astype(F32)
    q_dec_f = jnp.exp(lgf_lane * (row + 1.0))
    q_dec_b = jnp.exp(lgb_lane * (C - row))
    k_dec_f = jnp.exp(lgf_lane * (C - 1.0 - row))
    k_dec_b = jnp.exp(lgb_lane * row)
    diff = (iota((C, W), 0) - iota((C, W), 1) % C).astype(F32)
    decay = jnp.where(diff >= 0, jnp.exp(lgf_col * jnp.maximum(diff, 0.0)),
                      jnp.exp(lgb_col * jnp.maximum(-diff, 0.0)))
    chunk_dec_f = jnp.exp(lgf_col * C)
    chunk_dec_b = jnp.exp(lgb_col * C)
    own = (iota((LANES, W), 0) // (RET_QK_DIM // 2)) % 2 == iota((LANES, W), 1) // Dv
    h0_lane = lane_head == 0
    h0_col = col_head == 0

    def chunk(n):
        return pl.ds(pl.multiple_of(n * C, C), C)

    def kv(n, k_dec):
        return _dot_tn((rk_ref[0, chunk(n), :].astype(F32) * k_dec).astype(BF16), rv_ref[0, chunk(n), :])

    def state_body(i, carry):
        sf, sb = carry
        fwd = [i * unroll + u for u in range(unroll)]
        bwd = [n_chunks - 1 - t for t in fwd]
        kv_f = [kv(t, k_dec_f) for t in fwd]
        kv_b = [kv(n, k_dec_b) for n in bwd]
        for u in range(unroll):
            sf_scr[fwd[u]] = sf.astype(BF16)
            sb_scr[bwd[u]] = sb.astype(BF16)
            sf = sf * chunk_dec_f + jnp.where(own, kv_f[u], 0.0)
            sb = sb * chunk_dec_b + jnp.where(own, kv_b[u], 0.0)
        return sf, sb

    zero = jnp.zeros((LANES, W), F32)
    lax.fori_loop(0, n_chunks // unroll, state_body, (zero, zero))

    def scores(n):
        k = rk_ref[0, chunk(n), :]
        k_heads = jnp.concatenate([jnp.where(h0_lane, k, jnp.zeros_like(k)),
                                   jnp.where(h0_lane, jnp.zeros_like(k), k)], axis=0)
        return _dot_nt(rq_ref[0, chunk(n), :], k_heads)

    def outputs(n, s):
        qf = rq_ref[0, chunk(n), :].astype(F32)
        v = rv_ref[0, chunk(n), :]
        lhs = jnp.concatenate([(s * decay).astype(BF16), (qf * q_dec_f).astype(BF16),
                               (qf * q_dec_b).astype(BF16)], axis=1)
        v_heads = jnp.concatenate([jnp.where(h0_col, v, jnp.zeros_like(v)),
                                   jnp.where(h0_col, jnp.zeros_like(v), v)], axis=0)
        return _dot(lhs, jnp.concatenate([v_heads, sf_scr[n], sb_scr[n]], axis=0))

    def out_body(i, carry):
        chunks = [i * unroll + u for u in range(unroll)]
        s = [scores(n) for n in chunks]
        o = [outputs(n, s_n) for n, s_n in zip(chunks, s)]
        for n, o_n in zip(chunks, o):
            normed = jnp.concatenate([_rms(o_n[:, :Dv]), _rms(o_n[:, Dv:])], axis=1)
            o_ref[0, chunk(n), :] = (normed * rg_ref[0, chunk(n), :].astype(F32)).astype(BF16)
        return carry

    lax.fori_loop(0, n_chunks // unroll, out_body, 0)


def _ret_stage(lgf, lgb, rq, rk, rv, rg):
    B, S, _ = rq.shape
    qk_spec = pl.BlockSpec((1, S, LANES), lambda b, p: (b, 0, p))
    v_spec = pl.BlockSpec((1, S, 2 * RET_V_DIM), lambda b, p: (b, 0, p))
    smem = pl.BlockSpec(memory_space=pltpu.SMEM)
    state_scr = pltpu.VMEM((S // CHUNK, LANES, 2 * RET_V_DIM), BF16)
    assert (S // CHUNK) % RET_UNROLL == 0
    return pl.pallas_call(
        functools.partial(_ret_kernel, unroll=RET_UNROLL),
        grid=(B, RET_HEADS // 2),
        in_specs=[smem, smem, qk_spec, qk_spec, v_spec, v_spec],
        out_specs=v_spec,
        out_shape=jax.ShapeDtypeStruct((B, S, RET_HEADS * RET_V_DIM), BF16),
        scratch_shapes=[state_scr, state_scr],
        compiler_params=pltpu.CompilerParams(
            dimension_semantics=("parallel", "parallel"), vmem_limit_bytes=VMEM_LIMIT),
        name="ret_stage",
    )(lgf, lgb, rq, rk, rv, rg)


def _post_kernel(x_ref, at_ref, r_ref, sa_ref, sb_ref, wa_ref, wb_ref, wo_ref, gpm_ref,
                 gmlp_ref, wup_ref, wdn_ref, gpost_ref, y_ref):
    tm = x_ref.shape[1]
    ts = tm // POST_SPLIT
    parts = [slice(i * ts, (i + 1) * ts) for i in range(POST_SPLIT)]
    a = [_dot_tn(at_ref[0, :, :, r].reshape(MLA_HEADS * MLA_V_DIM, ts), wa_ref[...]) for r in parts]
    b = [_dot(r_ref[0, r, :], wb_ref[...]) for r in parts]
    merged = [(sa_ref[0, r, :].astype(F32) * a_i + sb_ref[0, r, :].astype(F32) * b_i).astype(BF16)
              for r, a_i, b_i in zip(parts, a, b)]
    m = [_dot(g, wo_ref[...]) for g in merged]
    x1 = [x_ref[0, r, :] + _rms(m_i, gpm_ref[...]) for r, m_i in zip(parts, m)]
    h = [_rms(x_i, gmlp_ref[...]).astype(BF16) for x_i in x1]
    u = [jnp.maximum(_dot(h_i, wup_ref[...]), 0.0) for h_i in h]
    d = [_dot((u_i * u_i).astype(BF16), wdn_ref[...]) for u_i in u]
    for r, x_i, d_i in zip(parts, x1, d):
        y_ref[0, r, :] = x_i + _rms(d_i, gpost_ref[...])


def _post_stage(x, at, r, sa, sb, w):
    B, S, D = x.shape
    tm = min(TM_POST, S)
    tok = lambda n: pl.BlockSpec((1, tm, n), lambda b, i: (b, i, 0))
    consts = [w["w_branch_a"], w["w_branch_b"], w["w_out"], w["g_post_mix"],
              w["g_pre_mlp"], w["w_up"], w["w_down"], w["g_post_mlp"]]
    return pl.pallas_call(
        _post_kernel,
        grid=(B, S // tm),
        in_specs=[tok(D), pl.BlockSpec((1, MLA_HEADS, MLA_V_DIM, tm), lambda b, i: (b, 0, 0, i)),
                  tok(RET_HEADS * RET_V_DIM), tok(D), tok(D)] + [_const_spec(c.shape) for c in consts],
        out_specs=tok(D),
        out_shape=jax.ShapeDtypeStruct((B, S, D), F32),
        compiler_params=pltpu.CompilerParams(
            dimension_semantics=("parallel", "parallel"), vmem_limit_bytes=VMEM_LIMIT),
        name="post_stage",
    )(x, at, r, sa, sb, *consts)


def _prepare_weights(g_pre_mix, w_in, g_q_norm, w_q_up, g_kv_norm, w_kv_up, w_branch_a, w_branch_b,
                     w_out, g_post_mix, g_pre_mlp, w_up, w_down, g_post_mlp):
    D = D_MODEL
    splits = (Q_LORA, KV_LORA, ROPE_DIM, RET_HEADS * RET_QK_DIM, RET_HEADS * RET_QK_DIM,
              RET_HEADS * RET_V_DIM, RET_HEADS * RET_V_DIM, D, D)
    offs = np.cumsum(splits)[:-1].tolist()
    w_cq, w_ckv, w_kr, w_rq, w_rk, w_rv, w_rg, w_ga, w_gb = jnp.split(w_in, offs, axis=-1)
    half = ROPE_DIM // 2
    w_kr_sw = jnp.concatenate([w_kr[:, half:], w_kr[:, :half]], axis=1)
    w_small = jnp.concatenate([w_cq, w_ckv, w_kr, w_kr_sw, jnp.zeros((D, LANES - 2 * ROPE_DIM), F32)], axis=1)

    rh = RET_QK_DIM // 2
    perm = np.array([RET_QK_DIM * (2 * p + hh) + rh * part + r
                     for p in range(RET_HEADS // 2) for part in range(2) for hh in range(2) for r in range(rh)])

    wq = w_q_up.reshape(Q_LORA, MLA_HEADS, NOPE_DIM + ROPE_DIM)
    w_qn_t = wq[:, :, :NOPE_DIM].reshape(Q_LORA, MLA_HEADS * NOPE_DIM).T
    w_qr_t = wq[:, :, NOPE_DIM:].reshape(Q_LORA, MLA_HEADS * ROPE_DIM).T

    wkv = w_kv_up.reshape(KV_LORA, MLA_HEADS, NOPE_DIM + MLA_V_DIM)
    k_nope = jnp.pad(wkv[:, :, :NOPE_DIM], ((0, 0), (0, 0), (0, QK_PAD - NOPE_DIM)))
    rope_place = jnp.pad(jnp.eye(ROPE_DIM, dtype=F32), ((0, 0), (NOPE_DIM, QK_PAD - NOPE_DIM - ROPE_DIM)))
    rope_rows = jnp.broadcast_to(rope_place[:, None, :], (ROPE_DIM, MLA_HEADS, QK_PAD))
    w_k = jnp.concatenate(
        [k_nope, rope_rows, jnp.zeros((LANES - ROPE_DIM, MLA_HEADS, QK_PAD), F32)], axis=0
    ).reshape(KV_LORA + LANES, MLA_HEADS * QK_PAD)
    w_v_t = wkv[:, :, NOPE_DIM:].reshape(KV_LORA, MLA_HEADS * MLA_V_DIM).T

    bf = lambda a: a.astype(BF16)
    row = lambda g: g.reshape(1, -1)
    return dict(
        g_pre_mix=row(g_pre_mix), w_small=bf(w_small), w_rq=bf(w_rq[:, perm]), w_rk=bf(w_rk[:, perm]),
        w_rv=bf(w_rv), w_rg=bf(w_rg), w_ga=bf(w_ga), w_gb=bf(w_gb),
        g_q_norm=row(g_q_norm), w_qn_t=bf(w_qn_t), w_qr_t=bf(w_qr_t), g_kv_norm=row(g_kv_norm),
        w_k=bf(w_k), w_v_t=bf(w_v_t),
        w_branch_a=bf(w_branch_a), w_branch_b=bf(w_branch_b), w_out=bf(w_out), g_post_mix=row(g_post_mix),
        g_pre_mlp=row(g_pre_mlp), w_up=bf(w_up), w_down=bf(w_down), g_post_mlp=row(g_post_mlp),
    )


def _trunk(x, w, lgf, lgb):
    qt, k, vt, rq, rk, rv, rg, sa, sb = _pre_stage(x, w)
    at = _attn_stage(qt, k, vt)
    r = _ret_stage(lgf, lgb, rq, rk, rv, rg)
    return _post_stage(x, at, r, sa, sb, w)


def kernel(x_prompt, x_sample, g_pre_mix, w_in, g_q_norm, w_q_up, g_kv_norm, w_kv_up, w_branch_a,
           ret_log_decay_fwd, ret_log_decay_bwd, w_branch_b, w_out, g_post_mix, g_pre_mlp, w_up, w_down,
           g_post_mlp):
    assert w_in.shape[0] == 1, "single-layer trunk"
    w = _prepare_weights(g_pre_mix[0], w_in[0], g_q_norm[0], w_q_up[0], g_kv_norm[0], w_kv_up[0],
                         w_branch_a[0], w_branch_b[0], w_out[0], g_post_mix[0], g_pre_mlp[0],
                         w_up[0], w_down[0], g_post_mlp[0])
    lgf = ret_log_decay_fwd[0].astype(F32)
    lgb = ret_log_decay_bwd[0].astype(F32)
    return _trunk(x_prompt, w, lgf, lgb), _trunk(x_sample, w, lgf, lgb)
```

```python
import functools

import jax
import jax.numpy as jnp
import numpy as np
from jax import lax
from jax.experimental import pallas as pl
from jax.experimental.pallas import tpu as pltpu

F32 = jnp.float32
BF16 = jnp.bfloat16

D_MODEL = 1024
MLA_HEADS = 8
Q_LORA = 256
KV_LORA = 128
NOPE_DIM = 64
ROPE_DIM = 32
MLA_V_DIM = 64
RET_HEADS = 8
RET_QK_DIM = 64
RET_V_DIM = 128
CHUNK = 128
D_FF = 4 * D_MODEL
ROPE_THETA = 10000.0
EPS = 1e-6

LANES = 128
QK_PAD = LANES
SMALL_N = Q_LORA + KV_LORA + LANES
VMEM_LIMIT = 56 * 1024 * 1024

VT_ROWS = MLA_V_DIM + 16

TM_PRE = 512
PRE_SPLIT = 2
TM_POST = 512
POST_SPLIT = 2
ATTN_TQ = 256
ATTN_TKV = 512
ATTN_MAX_CHUNKS = 16
ATTN_BODY_KEYS = 16384
ATTN_STEP_KEYS = 8192
ATTN_SKEW = 2
RET_UNROLL = 16


def _dot(a, b):
    return jnp.dot(a, b, preferred_element_type=F32)


def _dot_nt(a, b):
    return lax.dot_general(a, b, (((1,), (1,)), ((), ())), preferred_element_type=F32)


def _dot_tn(a, b):
    return lax.dot_general(a, b, (((0,), (0,)), ((), ())), preferred_element_type=F32)


def _rms(x, g=None):
    y = x * lax.rsqrt(jnp.mean(x * x, axis=-1, keepdims=True) + EPS)
    if g is not None:
        y = y * g
    return y


def _const_spec(shape):
    nd = len(shape)
    return pl.BlockSpec(shape, lambda *_: (0,) * nd, pipeline_mode=pl.Buffered(1))


def _pre_kernel(x_ref, g_ref, wsm_ref, wrq_ref, wrk_ref, wrv_ref, wrg_ref, wga_ref, wgb_ref,
                gq_ref, wqn_ref, wqr_ref, gkv_ref, wk_ref, wv_ref,
                cos_t_ref, sin_t_ref, cos_k_ref, sin_k_ref, cos_r_ref, sin_r_ref,
                qt_ref, k_ref, vt_ref, rq_ref, rk_ref, rv_ref, rg_ref, sa_ref, sb_ref):
    tm = x_ref.shape[1]
    ts = tm // PRE_SPLIT
    parts = [slice(i * ts, (i + 1) * ts) for i in range(PRE_SPLIT)]
    each = lambda fn, *lists: [fn(*args) for args in zip(*lists)]

    h = [_rms(x_ref[0, r, :], g_ref[...]).astype(BF16) for r in parts]
    sm = each(lambda a: _dot(a, wsm_ref[...]), h)
    rq = each(lambda a: _dot(a, wrq_ref[...]), h)
    rk = each(lambda a: _dot(a, wrk_ref[...]) * (RET_QK_DIM ** -0.5), h)
    rv = each(lambda a: _dot(a, wrv_ref[...]), h)

    cqn = each(lambda a: _rms(a[:, :Q_LORA], gq_ref[...]).astype(BF16), sm)
    ckvn = each(lambda a: _rms(a[:, Q_LORA:Q_LORA + KV_LORA], gkv_ref[...]).astype(BF16), sm)
    kr = [a[:, Q_LORA + KV_LORA:] for a in sm]
    k_rope = [(t * cos_k_ref[r, :] + pltpu.roll(t, LANES - ROPE_DIM, 1) * sin_k_ref[r, :]).astype(BF16)
              for t, r in zip(kr, parts)]

    scale = (NOPE_DIM + ROPE_DIM) ** -0.5 * np.log2(np.e)
    qn = each(lambda a: _dot_nt(wqn_ref[...], a) * scale, cqn)
    qr = each(lambda a: _dot_nt(wqr_ref[...], a) * scale, cqn)
    kk = each(lambda a, b: _dot(jnp.concatenate([a, b], axis=1), wk_ref[...]), ckvn, k_rope)
    vt = each(lambda a: _dot_nt(wv_ref[...], a), ckvn)
    rg = each(lambda a: _dot(a, wrg_ref[...]), h)
    ga = each(lambda a: _dot(a, wga_ref[...]), h)
    gb = each(lambda a: _dot(a, wgb_ref[...]), h)

    half = ROPE_DIM // 2
    cos_r, sin_r = cos_r_ref[...], sin_r_ref[...]
    for i, r in enumerate(parts):
        for p in range(RET_HEADS // 2):
            sl = slice(p * LANES, (p + 1) * LANES)
            bq, bk = rq[i][:, sl], rk[i][:, sl]
            rq_ref[0, r, sl] = (bq * cos_r[r] + pltpu.roll(bq, LANES // 2, 1) * sin_r[r]).astype(BF16)
            rk_ref[0, r, sl] = (bk * cos_r[r] + pltpu.roll(bk, LANES // 2, 1) * sin_r[r]).astype(BF16)
        rv_ref[0, r, :] = rv[i].astype(BF16)

        q_rope = qr[i].reshape(MLA_HEADS, ROPE_DIM, ts)
        x1, x2 = q_rope[:, :half], q_rope[:, half:]
        cos_t, sin_t = cos_t_ref[:, r][None], sin_t_ref[:, r][None]
        qt = jnp.concatenate(
            [qn[i].reshape(MLA_HEADS, NOPE_DIM, ts), x1 * cos_t - x2 * sin_t, x1 * sin_t + x2 * cos_t,
             jnp.zeros((MLA_HEADS, QK_PAD - NOPE_DIM - ROPE_DIM, ts), F32)], axis=1)
        qt_ref[0, :, :, r] = qt.astype(BF16)
        k_ref[0, r, :] = kk[i].astype(BF16)
        ones = jnp.ones((MLA_HEADS, VT_ROWS - MLA_V_DIM, ts), F32)
        vt_ref[0, :, :, r] = jnp.concatenate([vt[i].reshape(MLA_HEADS, MLA_V_DIM, ts), ones], axis=1).astype(BF16)

        rg_ref[0, r, :] = (rg[i] * jax.nn.sigmoid(rg[i])).astype(BF16)
        sa_ref[0, r, :] = jax.nn.sigmoid(ga[i]).astype(BF16)
        sb_ref[0, r, :] = jax.nn.sigmoid(gb[i]).astype(BF16)


def _pre_stage(x, w):
    B, S, D = x.shape
    tm = min(TM_PRE, S)
    grid = (B, S // tm)
    tok = lambda n: pl.BlockSpec((1, tm, n), lambda b, i: (b, i, 0))
    tab = lambda n: pl.BlockSpec((tm, n), lambda b, i: (i, 0))
    tab_t = pl.BlockSpec((ROPE_DIM // 2, tm), lambda b, i: (0, i))
    head_t = lambda n: pl.BlockSpec((1, MLA_HEADS, n, tm), lambda b, i: (b, 0, 0, i))
    consts = [w["g_pre_mix"], w["w_small"], w["w_rq"], w["w_rk"], w["w_rv"], w["w_rg"], w["w_ga"], w["w_gb"],
              w["g_q_norm"], w["w_qn_t"], w["w_qr_t"], w["g_kv_norm"], w["w_k"], w["w_v_t"]]
    cos_t, sin_t, cos_k, sin_k, cos_r, sin_r = _rope_tables(S)
    out_shape = [
        jax.ShapeDtypeStruct((B, MLA_HEADS, QK_PAD, S), BF16),
        jax.ShapeDtypeStruct((B, S, MLA_HEADS * QK_PAD), BF16),
        jax.ShapeDtypeStruct((B, MLA_HEADS, VT_ROWS, S), BF16),
        jax.ShapeDtypeStruct((B, S, RET_HEADS * RET_QK_DIM), BF16),
        jax.ShapeDtypeStruct((B, S, RET_HEADS * RET_QK_DIM), BF16),
        jax.ShapeDtypeStruct((B, S, RET_HEADS * RET_V_DIM), BF16),
        jax.ShapeDtypeStruct((B, S, RET_HEADS * RET_V_DIM), BF16),
        jax.ShapeDtypeStruct((B, S, D), BF16),
        jax.ShapeDtypeStruct((B, S, D), BF16),
    ]
    out_specs = [head_t(QK_PAD), tok(MLA_HEADS * QK_PAD), head_t(VT_ROWS),
                 tok(RET_HEADS * RET_QK_DIM), tok(RET_HEADS * RET_QK_DIM),
                 tok(RET_HEADS * RET_V_DIM), tok(RET_HEADS * RET_V_DIM), tok(D), tok(D)]
    return pl.pallas_call(
        _pre_kernel,
        grid=grid,
        in_specs=[tok(D)] + [_const_spec(c.shape) for c in consts]
                 + [tab_t, tab_t, tab(LANES), tab(LANES), tab(LANES), tab(LANES)],
        out_specs=out_specs,
        out_shape=out_shape,
        compiler_params=pltpu.CompilerParams(
            dimension_semantics=("parallel", "parallel"), vmem_limit_bytes=VMEM_LIMIT),
        name="pre_stage",
    )(x, *consts, cos_t, sin_t, cos_k, sin_k, cos_r, sin_r)


def _rope_angles(seq, dim):
    inv = 1.0 / (ROPE_THETA ** (jnp.arange(0, dim, 2, dtype=F32) / dim))
    ang = jnp.arange(seq, dtype=F32)[:, None] * inv[None, :]
    return jnp.cos(ang), jnp.sin(ang)


def _rope_tables(S):
    cos_m, sin_m = _rope_angles(S, ROPE_DIM)
    zpad = jnp.zeros((S, LANES - ROPE_DIM), F32)
    cos_k = jnp.concatenate([cos_m, cos_m, zpad], axis=1)
    sin_k = jnp.concatenate([-sin_m, sin_m, zpad], axis=1)
    cos_r, sin_r = _rope_angles(S, RET_QK_DIM)
    cos_r4 = jnp.concatenate([cos_r] * 4, axis=1)
    sin_r4 = jnp.concatenate([-sin_r, -sin_r, sin_r, sin_r], axis=1)
    return cos_m.T, sin_m.T, cos_k, sin_k, cos_r4, sin_r4


def _attn_kernel(qt_ref, k_ref, vt_ref, o_ref, s0_scr, s1_scr, *, tkv, tqs, skew):
    S = k_ref.shape[1]
    n_kv = S // tkv
    n_q = S // tqs
    n_tiles = qt_ref.shape[1] * n_q
    assert n_tiles % 2 == 0 and n_kv >= skew
    sub = 8
    mx_init = jnp.full((sub, tqs), -jnp.inf, F32)
    acc_init = jnp.zeros((VT_ROWS, tqs), F32)

    def head_of(t):
        return t // n_q

    def q_cols(t):
        return pl.ds(pl.multiple_of((t % n_q) * tqs, tqs), tqs)

    def kv_rows(c):
        return pl.ds(c * tkv, tkv)

    def pass_a(c, t, qt, dst, mx):
        k = k_ref[0, kv_rows(c), pl.ds(pl.multiple_of(head_of(t) * QK_PAD, QK_PAD), QK_PAD)]
        s = _dot(k, qt)
        dst[kv_rows(c), :] = s
        return jnp.maximum(mx, jnp.max(s.reshape(tkv // sub, sub, tqs), axis=0))

    def pass_b(c, t, m, src, acc):
        p = jnp.exp2(src[kv_rows(c), :] - m).astype(BF16)
        return acc + _dot(vt_ref[0, head_of(t), :, kv_rows(c)], p)

    def col_max(mx):
        return jnp.max(mx, axis=0, keepdims=True)

    def finish(t, acc):
        o_ref[0, head_of(t), :, q_cols(t)] = (acc[:MLA_V_DIM] / acc[MLA_V_DIM:MLA_V_DIM + 1]).astype(BF16)

    def scores_only(t, dst):
        qt = qt_ref[0, head_of(t), :, q_cols(t)]
        mx = mx_init
        for c in range(n_kv):
            mx = pass_a(c, t, qt, dst, mx)
        return col_max(mx)

    def softmax_and_next_scores(t, m, src, dst):
        qt_next = qt_ref[0, head_of(t + 1), :, q_cols(t + 1)]
        acc, mx = acc_init, mx_init
        for c in range(n_kv + skew):
            if c < n_kv:
                mx = pass_a(c, t + 1, qt_next, dst, mx)
            if c >= skew:
                acc = pass_b(c - skew, t, m, src, acc)
        finish(t, acc)
        return col_max(mx)

    def softmax_only(t, m, src):
        acc = acc_init
        for c in range(n_kv):
            acc = pass_b(c, t, m, src, acc)
        finish(t, acc)

    def two_tiles(i, m):
        m = softmax_and_next_scores(2 * i, m, s0_scr, s1_scr)
        return softmax_and_next_scores(2 * i + 1, m, s1_scr, s0_scr)

    unroll = max(2, ATTN_BODY_KEYS // (2 * S))
    m = lax.fori_loop(0, n_tiles // 2 - 1, two_tiles, scores_only(0, s0_scr), unroll=unroll)
    m = softmax_and_next_scores(n_tiles - 2, m, s0_scr, s1_scr)
    softmax_only(n_tiles - 1, m, s1_scr)


def _attn_stage(qt, k, vt):
    B, H, _, S = qt.shape
    tqs = min(ATTN_TQ, S)
    tkv = max(min(ATTN_TKV, S), S // ATTN_MAX_CHUNKS)
    g = max(1,min(H, ATTN_STEP_KEYS // S))
    return pl.pallas_call(
        functools.partial(_attn_kernel, tkv=tkv, tqs=tqs, skew=ATTN_SKEW),
        grid=(B, H // g),
        in_specs=[
            pl.BlockSpec((1, g, QK_PAD, S), lambda b, h: (b, h, 0, 0)),
            pl.BlockSpec((1, S, g * QK_PAD), lambda b, h: (b, 0, h)),
            pl.BlockSpec((1, g, VT_ROWS, S), lambda b, h: (b, h, 0, 0)),
        ],
        out_specs=pl.BlockSpec((1, g, MLA_V_DIM, S), lambda b, h: (b, h, 0, 0)),
        out_shape=jax.ShapeDtypeStruct((B, H, MLA_V_DIM, S), BF16),
        scratch_shapes=[pltpu.VMEM((S, tqs), F32), pltpu.VMEM((S, tqs), F32)],
        compiler_params=pltpu.CompilerParams(
            dimension_semantics=("parallel", "parallel"), vmem_limit_bytes=VMEM_LIMIT),
        name="attn_stage",
    )(qt, k, vt)


def _ret_kernel(lgf_ref, lgb_ref, rq_ref, rk_ref, rv_ref, rg_ref, o_ref, sf_scr, sb_scr, *, unroll):
    C, Dv = CHUNK, RET_V_DIM
    W = 2 * Dv
    n_chunks = rq_ref.shape[1] // C
    pair = pl.program_id(1)

    def iota(shape, dim):
        return lax.broadcasted_iota(jnp.int32, shape, dim)

    def rate(head_idx, ref):
        return jnp.where(head_idx == 0, ref[2 * pair], ref[2 * pair + 1])

    lane_head = (iota((1, LANES), 1) // (RET_QK_DIM // 2)) % 2
    col_head = iota((1, W), 1) // Dv
    lgf_lane, lgb_lane = rate(lane_head, lgf_ref), rate(lane_head, lgb_ref)
    lgf_col, lgb_col = rate(col_head, lgf_ref), rate(col_head, lgb_ref)
    row = iota((C, LANES), 0).astype(F32)
    q_dec_f = jnp.exp(lgf_lane * (row + 1.0))
    q_dec_b = jnp.exp(lgb_lane * (C - row))
    k_dec_f = jnp.exp(lgf_lane * (C - 1.0 - row))
    k_dec_b = jnp.exp(lgb_lane * row)
    diff = (iota((C, W), 0) - iota((C, W), 1) % C).astype(F32)
    decay = jnp.where(diff >= 0, jnp.exp(lgf_col * jnp.maximum(diff, 0.0)),
                      jnp.exp(lgb_col * jnp.maximum(-diff, 0.0)))
    chunk_dec_f = jnp.exp(lgf_col * C)
    chunk_dec_b = jnp.exp(lgb_col * C)
    own = (iota((LANES, W), 0) // (RET_QK_DIM // 2)) % 2 == iota((LANES, W), 1) // Dv
    h0_lane = lane_head == 0
    h0_col = col_head == 0

    def chunk(n):
        return pl.ds(pl.multiple_of(n * C, C), C)

    def kv(n, k_dec):
        return _dot_tn((rk_ref[0, chunk(n), :].astype(F32) * k_dec).astype(BF16), rv_ref[0, chunk(n), :])

    def state_body(i, carry):
        sf, sb = carry
        fwd = [i * unroll + u for u in range(unroll)]
        bwd = [n_chunks - 1 - t for t in fwd]
        kv_f = [kv(t, k_dec_f) for t in fwd]
        kv_b = [kv(n, k_dec_b) for n in bwd]
        for u in range(unroll):
            sf_scr[fwd[u]] = sf.astype(BF16)
            sb_scr[bwd[u]] = sb.astype(BF16)
            sf = sf * chunk_dec_f + jnp.where(own, kv_f[u], 0.0)
            sb = sb * chunk_dec_b + jnp.where(own, kv_b[u], 0.0)
        return sf, sb

    zero = jnp.zeros((LANES, W), F32)
    lax.fori_loop(0, n_chunks // unroll, state_body, (zero, zero))

    def scores(n):
        k = rk_ref[0, chunk(n), :]
        k_heads = jnp.concatenate([jnp.where(h0_lane, k, jnp.zeros_like(k)),
                                   jnp.where(h0_lane, jnp.zeros_like(k), k)], axis=0)
        return _dot_nt(rq_ref[0, chunk(n), :], k_heads)

    def outputs(n, s):
        qf = rq_ref[0, chunk(n), :].astype(F32)
        v = rv_ref[0, chunk(n), :]
        lhs = jnp.concatenate([(s * decay).astype(BF16), (qf * q_dec_f).astype(BF16),
                               (qf * q_dec_b).astype(BF16)], axis=1)
        v_heads = jnp.concatenate([jnp.where(h0_col, v, jnp.zeros_like(v)),
                                   jnp.where(h0_col, jnp.zeros_like(v), v)], axis=0)
        return _dot(lhs, jnp.concatenate([v_heads, sf_scr[n], sb_scr[n]], axis=0))

    def out_body(i, carry):
        chunks = [i * unroll + u for u in range(unroll)]
        s = [scores(n) for n in chunks]
        o = [outputs(n, s_n) for n, s_n in zip(chunks, s)]
        for n, o_n in zip(chunks, o):
            normed = jnp.concatenate([_rms(o_n[:, :Dv]), _rms(o_n[:, Dv:])], axis=1)
            o_ref[0, chunk(n), :] = (normed * rg_ref[0, chunk(n), :].astype(F32)).astype(BF16)
        return carry

    lax.fori_loop(0, n_chunks // unroll, out_body, 0)


def _ret_stage(lgf, lgb, rq, rk, rv, rg):
    B, S, _ = rq.shape
    qk_spec = pl.BlockSpec((1, S, LANES), lambda b, p: (b, 0, p))
    v_spec = pl.BlockSpec((1, S, 2 * RET_V_DIM), lambda b, p: (b, 0, p))
    smem = pl.BlockSpec(memory_space=pltpu.SMEM)
    state_scr = pltpu.VMEM((S // CHUNK, LANES, 2 * RET_V_DIM), BF16)
    assert (S // CHUNK) % RET_UNROLL == 0
    return pl.pallas_call(
        functools.partial(_ret_kernel, unroll=RET_UNROLL),
        grid=(B, RET_HEADS // 2),
        in_specs=[smem, smem, qk_spec, qk_spec, v_spec, v_spec],
        out_specs=v_spec,
        out_shape=jax.ShapeDtypeStruct((B, S, RET_HEADS * RET_V_DIM), BF16),
        scratch_shapes=[state_scr, state_scr],
        compiler_params=pltpu.CompilerParams(
            dimension_semantics=("parallel", "parallel"), vmem_limit_bytes=VMEM_LIMIT),
        name="ret_stage",
    )(lgf, lgb, rq, rk, rv, rg)


def _post_kernel(x_ref, at_ref, r_ref, sa_ref, sb_ref, wa_ref, wb_ref, wo_ref, gpm_ref,
                 gmlp_ref, wup_ref, wdn_ref, gpost_ref, y_ref):
    tm = x_ref.shape[1]
    ts = tm // POST_SPLIT
    parts = [slice(i * ts, (i + 1) * ts) for i in range(POST_SPLIT)]
    a = [_dot_tn(at_ref[0, :, :, r].reshape(MLA_HEADS * MLA_V_DIM, ts), wa_ref[...]) for r in parts]
    b = [_dot(r_ref[0, r, :], wb_ref[...]) for r in parts]
    merged = [(sa_ref[0, r, :].astype(F32) * a_i + sb_ref[0, r, :].astype(F32) * b_i).astype(BF16)
              for r, a_i, b_i in zip(parts, a, b)]
    m = [_dot(g, wo_ref[...]) for g in merged]
    x1 = [x_ref[0, r, :] + _rms(m_i, gpm_ref[...]) for r, m_i in zip(parts, m)]
    h = [_rms(x_i, gmlp_ref[...]).astype(BF16) for x_i in x1]
    u = [jnp.maximum(_dot(h_i, wup_ref[...]), 0.0) for h_i in h]
    d = [_dot((u_i * u_i).astype(BF16), wdn_ref[...]) for u_i in u]
    for r, x_i, d_i in zip(parts, x1, d):
        y_ref[0, r, :] = x_i + _rms(d_i, gpost_ref[...])


def _post_stage(x, at, r, sa, sb, w):
    B, S, D = x.shape
    tm = min(TM_POST, S)
    tok = lambda n: pl.BlockSpec((1, tm, n), lambda b, i: (b, i, 0))
    consts = [w["w_branch_a"], w["w_branch_b"], w["w_out"], w["g_post_mix"],
              w["g_pre_mlp"], w["w_up"], w["w_down"], w["g_post_mlp"]]
    return pl.pallas_call(
        _post_kernel,
        grid=(B, S // tm),
        in_specs=[tok(D), pl.BlockSpec((1, MLA_HEADS, MLA_V_DIM, tm), lambda b, i: (b, 0, 0, i)),
                  tok(RET_HEADS * RET_V_DIM), tok(D), tok(D)] + [_const_spec(c.shape) for c in consts],
        out_specs=tok(D),
        out_shape=jax.ShapeDtypeStruct((B, S, D), F32),
        compiler_params=pltpu.CompilerParams(
            dimension_semantics=("parallel", "parallel"), vmem_limit_bytes=VMEM_LIMIT),
        name="post_stage",
    )(x, at, r, sa, sb, *consts)


def _prepare_weights(g_pre_mix, w_in, g_q_norm, w_q_up, g_kv_norm, w_kv_up, w_branch_a, w_branch_b,
                     w_out, g_post_mix, g_pre_mlp, w_up, w_down, g_post_mlp):
    D = D_MODEL
    splits = (Q_LORA, KV_LORA, ROPE_DIM, RET_HEADS * RET_QK_DIM, RET_HEADS * RET_QK_DIM,
              RET_HEADS * RET_V_DIM, RET_HEADS * RET_V_DIM, D, D)
    offs = np.cumsum(splits)[:-1].tolist()
    w_cq, w_ckv, w_kr, w_rq, w_rk, w_rv, w_rg, w_ga, w_gb = jnp.split(w_in, offs, axis=-1)
    half = ROPE_DIM // 2
    w_kr_sw = jnp.concatenate([w_kr[:, half:], w_kr[:, :half]], axis=1)
    w_small = jnp.concatenate([w_cq, w_ckv, w_kr, w_kr_sw, jnp.zeros((D, LANES - 2 * ROPE_DIM), F32)], axis=1)

    rh = RET_QK_DIM // 2
    perm = np.array([RET_QK_DIM * (2 * p + hh) + rh * part + r
                     for p in range(RET_HEADS // 2) for part in range(2) for hh in range(2) for r in range(rh)])

    wq = w_q_up.reshape(Q_LORA, MLA_HEADS, NOPE_DIM + ROPE_DIM)
    w_qn_t = wq[:, :, :NOPE_DIM].reshape(Q_LORA, MLA_HEADS * NOPE_DIM).T
    w_qr_t = wq[:, :, NOPE_DIM:].reshape(Q_LORA, MLA_HEADS * ROPE_DIM).T

    wkv = w_kv_up.reshape(KV_LORA, MLA_HEADS, NOPE_DIM + MLA_V_DIM)
    k_nope = jnp.pad(wkv[:, :, :NOPE_DIM], ((0, 0), (0, 0), (0, QK_PAD - NOPE_DIM)))
    rope_place = jnp.pad(jnp.eye(ROPE_DIM, dtype=F32), ((0, 0), (NOPE_DIM, QK_PAD - NOPE_DIM - ROPE_DIM)))
    rope_rows = jnp.broadcast_to(rope_place[:, None, :], (ROPE_DIM, MLA_HEADS, QK_PAD))
    w_k = jnp.concatenate(
        [k_nope, rope_rows, jnp.zeros((LANES - ROPE_DIM, MLA_HEADS, QK_PAD), F32)], axis=0
    ).reshape(KV_LORA + LANES, MLA_HEADS * QK_PAD)
    w_v_t = wkv[:, :, NOPE_DIM:].reshape(KV_LORA, MLA_HEADS * MLA_V_DIM).T

    bf = lambda a: a.astype(BF16)
    row = lambda g: g.reshape(1, -1)
    return dict(
        g_pre_mix=row(g_pre_mix), w_small=bf(w_small), w_rq=bf(w_rq[:, perm]), w_rk=bf(w_rk[:, perm]),
        w_rv=bf(w_rv), w_rg=bf(w_rg), w_ga=bf(w_ga), w_gb=bf(w_gb),
        g_q_norm=row(g_q_norm), w_qn_t=bf(w_qn_t), w_qr_t=bf(w_qr_t), g_kv_norm=row(g_kv_norm),
        w_k=bf(w_k), w_v_t=bf(w_v_t),
        w_branch_a=bf(w_branch_a), w_branch_b=bf(w_branch_b), w_out=bf(w_out), g_post_mix=row(g_post_mix),
        g_pre_mlp=row(g_pre_mlp), w_up=bf(w_up), w_down=bf(w_down), g_post_mlp=row(g_post_mlp),
    )


def _trunk(x, w, lgf, lgb):
    qt, k, vt, rq, rk, rv, rg, sa, sb = _pre_stage(x, w)
    at = _attn_stage(qt, k, vt)
    r = _ret_stage(lgf, lgb, rq, rk, rv, rg)
    return _post_stage(x, at, r, sa, sb, w)


def kernel(x_prompt, x_sample, g_pre_mix, w_in, g_q_norm, w_q_up, g_kv_norm, w_kv_up, w_branch_a,
           ret_log_decay_fwd, ret_log_decay_bwd, w_branch_b, w_out, g_post_mix, g_pre_mlp, w_up, w_down,
           g_post_mlp):
    assert w_in.shape[0] == 1, "single-layer trunk"
    w = _prepare_weights(g_pre_mix[0], w_in[0], g_q_norm[0], w_q_up[0], g_kv_norm[0], w_kv_up[0],
                         w_branch_a[0], w_branch_b[0], w_out[0], g_post_mix[0], g_pre_mlp[0],
                         w_up[0], w_down[0], g_post_mlp[0])
    lgf = ret_log_decay_fwd[0].astype(F32)
    lgb = ret_log_decay_bwd[0].astype(F32)
    return _trunk(x_prompt, w, lgf, lgb), _trunk(x_sample, w, lgf, lgb)
```

```python
import functools

import jax
import jax.numpy as jnp
import numpy as np
from jax import lax
from jax.experimental import pallas as pl
from jax.experimental.pallas import tpu as pltpu

F32 = jnp.float32
BF16 = jnp.bfloat16

D_MODEL = 1024
MLA_HEADS = 8
Q_LORA = 256
KV_LORA = 128
NOPE_DIM = 64
ROPE_DIM = 32
MLA_V_DIM = 64
RET_HEADS = 8
RET_QK_DIM = 64
RET_V_DIM = 128
CHUNK = 128
D_FF = 4 * D_MODEL
ROPE_THETA = 10000.0
EPS = 1e-6

LANES = 128
QK_PAD = LANES
SMALL_N = Q_LORA + KV_LORA + LANES
VMEM_LIMIT = 56 * 1024 * 1024

VT_ROWS = MLA_V_DIM + 16

TM_PRE = 512
PRE_SPLIT = 2
TM_POST = 512
POST_SPLIT = 2
ROPE_POS_LO = 128
ATTN_TQ = 256
ATTN_TKV = 512
ATTN_MAX_CHUNKS = 16
ATTN_BODY_KEYS = 16384
ATTN_STEP_KEYS = 8192
ATTN_SKEW = 2
RET_UNROLL = 16


def _dot(a, b):
    return jnp.dot(a, b, preferred_element_type=F32)


def _dot_nt(a, b):
    return lax.dot_general(a, b, (((1,), (1,)), ((), ())), preferred_element_type=F32)


def _dot_tn(a, b):
    return lax.dot_general(a, b, (((0,), (0,)), ((), ())), preferred_element_type=F32)


def _rms(x, g=None):
    y = x * lax.rsqrt(jnp.mean(x * x, axis=-1, keepdims=True) + EPS)
    if g is not None:
        y = y * g
    return y


def _const_spec(shape):
    nd = len(shape)
    return pl.BlockSpec(shape, lambda *_: (0,) * nd, pipeline_mode=pl.Buffered(1))


def _pre_kernel(x_ref, g_ref, wsm_ref, wrq_ref, wrk_ref, wrv_ref, wrg_ref, wga_ref, wgb_ref,
                gq_ref, wqn_ref, wqr_ref, gkv_ref, wk_ref, wv_ref,
                cos_t_ref, sin_t_ref, cos_k_ref, sin_k_ref, cos_r_ref, sin_r_ref,
                qt_ref, k_ref, vt_ref, rq_ref, rk_ref, rv_ref, rg_ref, sa_ref, sb_ref):
    tm = x_ref.shape[1]
    ts = tm // PRE_SPLIT
    parts = [slice(i * ts, (i + 1) * ts) for i in range(PRE_SPLIT)]
    each = lambda fn, *lists: [fn(*args) for args in zip(*lists)]

    h = [_rms(x_ref[0, r, :], g_ref[...]).astype(BF16) for r in parts]
    sm = each(lambda a: _dot(a, wsm_ref[...]), h)
    rq = each(lambda a: _dot(a, wrq_ref[...]), h)
    rk = each(lambda a: _dot(a, wrk_ref[...]) * (RET_QK_DIM ** -0.5), h)
    rv = each(lambda a: _dot(a, wrv_ref[...]), h)

    cqn = each(lambda a: _rms(a[:, :Q_LORA], gq_ref[...]).astype(BF16), sm)
    ckvn = each(lambda a: _rms(a[:, Q_LORA:Q_LORA + KV_LORA], gkv_ref[...]).astype(BF16), sm)
    kr = [a[:, Q_LORA + KV_LORA:] for a in sm]
    k_rope = [(t * cos_k_ref[r, :] + pltpu.roll(t, LANES - ROPE_DIM, 1) * sin_k_ref[r, :]).astype(BF16)
              for t, r in zip(kr, parts)]

    scale = (NOPE_DIM + ROPE_DIM) ** -0.5 * np.log2(np.e)
    qn = each(lambda a: _dot_nt(wqn_ref[...], a) * scale, cqn)
    qr = each(lambda a: _dot_nt(wqr_ref[...], a) * scale, cqn)
    kk = each(lambda a, b: _dot(jnp.concatenate([a, b], axis=1), wk_ref[...]), ckvn, k_rope)
    vt = each(lambda a: _dot_nt(wv_ref[...], a), ckvn)
    rg = each(lambda a: _dot(a, wrg_ref[...]), h)
    ga = each(lambda a: _dot(a, wga_ref[...]), h)
    gb = each(lambda a: _dot(a, wgb_ref[...]), h)

    half = ROPE_DIM // 2
    cos_r, sin_r = cos_r_ref[...], sin_r_ref[...]
    for i, r in enumerate(parts):
        for p in range(RET_HEADS // 2):
            sl = slice(p * LANES, (p + 1) * LANES)
            bq, bk = rq[i][:, sl], rk[i][:, sl]
            rq_ref[0, r, sl] = (bq * cos_r[r] + pltpu.roll(bq, LANES // 2, 1) * sin_r[r]).astype(BF16)
            rk_ref[0, r, sl] = (bk * cos_r[r] + pltpu.roll(bk, LANES // 2, 1) * sin_r[r]).astype(BF16)
        rv_ref[0, r, :] = rv[i].astype(BF16)

        q_rope = qr[i].reshape(MLA_HEADS, ROPE_DIM, ts)
        x1, x2 = q_rope[:, :half], q_rope[:, half:]
        cos_t, sin_t = cos_t_ref[:, r][None], sin_t_ref[:, r][None]
        qt = jnp.concatenate(
            [qn[i].reshape(MLA_HEADS, NOPE_DIM, ts), x1 * cos_t - x2 * sin_t, x1 * sin_t + x2 * cos_t,
             jnp.zeros((MLA_HEADS, QK_PAD - NOPE_DIM - ROPE_DIM, ts), F32)], axis=1)
        qt_ref[0, :, :, r] = qt.astype(BF16)
        k_ref[0, r, :] = kk[i].astype(BF16)
        ones = jnp.ones((MLA_HEADS, VT_ROWS - MLA_V_DIM, ts), F32)
        vt_ref[0, :, :, r] = jnp.concatenate([vt[i].reshape(MLA_HEADS, MLA_V_DIM, ts), ones], axis=1).astype(BF16)

        rg_ref[0, r, :] = (rg[i] * jax.nn.sigmoid(rg[i])).astype(BF16)
        sa_ref[0, r, :] = jax.nn.sigmoid(ga[i]).astype(BF16)
        sb_ref[0, r, :] = jax.nn.sigmoid(gb[i]).astype(BF16)


def _pre_stage(x, w, tables):
    B, S, D = x.shape
    tm = min(TM_PRE, S)
    grid = (B, S // tm)
    tok = lambda n: pl.BlockSpec((1, tm, n), lambda b, i: (b, i, 0))
    tab = lambda n: pl.BlockSpec((tm, n), lambda b, i: (i, 0))
    tab_t = pl.BlockSpec((ROPE_DIM // 2, tm), lambda b, i: (0, i))
    head_t = lambda n: pl.BlockSpec((1, MLA_HEADS, n, tm), lambda b, i: (b, 0, 0, i))
    consts = [w["g_pre_mix"], w["w_small"], w["w_rq"], w["w_rk"], w["w_rv"], w["w_rg"], w["w_ga"], w["w_gb"],
              w["g_q_norm"], w["w_qn_t"], w["w_qr_t"], w["g_kv_norm"], w["w_k"], w["w_v_t"]]
    cos_t, sin_t, cos_k, sin_k, cos_r, sin_r = tables
    out_shape = [
        jax.ShapeDtypeStruct((B, MLA_HEADS, QK_PAD, S), BF16),
        jax.ShapeDtypeStruct((B, S, MLA_HEADS * QK_PAD), BF16),
        jax.ShapeDtypeStruct((B, MLA_HEADS, VT_ROWS, S), BF16),
        jax.ShapeDtypeStruct((B, S, RET_HEADS * RET_QK_DIM), BF16),
        jax.ShapeDtypeStruct((B, S, RET_HEADS * RET_QK_DIM), BF16),
        jax.ShapeDtypeStruct((B, S, RET_HEADS * RET_V_DIM), BF16),
        jax.ShapeDtypeStruct((B, S, RET_HEADS * RET_V_DIM), BF16),
        jax.ShapeDtypeStruct((B, S, D), BF16),
        jax.ShapeDtypeStruct((B, S, D), BF16),
    ]
    out_specs = [head_t(QK_PAD), tok(MLA_HEADS * QK_PAD), head_t(VT_ROWS),
                 tok(RET_HEADS * RET_QK_DIM), tok(RET_HEADS * RET_QK_DIM),
                 tok(RET_HEADS * RET_V_DIM), tok(RET_HEADS * RET_V_DIM), tok(D), tok(D)]
    return pl.pallas_call(
        _pre_kernel,
        grid=grid,
        in_specs=[tok(D)] + [_const_spec(c.shape) for c in consts]
                 + [tab_t, tab_t, tab(LANES), tab(LANES), tab(LANES), tab(LANES)],
        out_specs=out_specs,
        out_shape=out_shape,
        compiler_params=pltpu.CompilerParams(
            dimension_semantics=("parallel", "parallel"), vmem_limit_bytes=VMEM_LIMIT),
        name="pre_stage",
    )(x, *consts, cos_t, sin_t, cos_k, sin_k, cos_r, sin_r)


def _rope_angles(seq, dim):
    inv = 1.0 / (ROPE_THETA ** (jnp.arange(0, dim, 2, dtype=F32) / dim))
    assert seq % ROPE_POS_LO == 0
    lo = jnp.arange(ROPE_POS_LO, dtype=F32)[:, None] * inv[None, :]
    hi = (jnp.arange(seq // ROPE_POS_LO, dtype=F32) * ROPE_POS_LO)[:, None] * inv[None, :]
    cos_lo, sin_lo, cos_hi, sin_hi = jnp.cos(lo)[None], jnp.sin(lo)[None], jnp.cos(hi)[:, None], jnp.sin(hi)[:, None]
    cos = (cos_hi * cos_lo - sin_hi * sin_lo).reshape(seq, dim // 2)
    sin = (sin_hi * cos_lo + cos_hi * sin_lo).reshape(seq, dim // 2)
    return cos, sin


def _rope_tables(S):
    cos_m, sin_m = _rope_angles(S, ROPE_DIM)
    zpad = jnp.zeros((S, LANES - ROPE_DIM), F32)
    cos_k = jnp.concatenate([cos_m, cos_m, zpad], axis=1)
    sin_k = jnp.concatenate([-sin_m, sin_m, zpad], axis=1)
    cos_r, sin_r = _rope_angles(S, RET_QK_DIM)
    cos_r4 = jnp.concatenate([cos_r] * 4, axis=1)
    sin_r4 = jnp.concatenate([-sin_r, -sin_r, sin_r, sin_r], axis=1)
    return cos_m.T, sin_m.T, cos_k, sin_k, cos_r4, sin_r4


def _attn_kernel(qt_ref, k_ref, vt_ref, o_ref, s0_scr, s1_scr, *, tkv, tqs, skew):
    S = k_ref.shape[1]
    n_kv = S // tkv
    n_q = S // tqs
    n_tiles = qt_ref.shape[1] * n_q
    assert n_tiles % 2 == 0 and n_kv >= skew
    sub = 8
    mx_init = jnp.full((sub, tqs), -jnp.inf, F32)
    acc_init = jnp.zeros((VT_ROWS, tqs), F32)

    def head_of(t):
        return t // n_q

    def q_cols(t):
        return pl.ds(pl.multiple_of((t % n_q) * tqs, tqs), tqs)

    def kv_rows(c):
        return pl.ds(c * tkv, tkv)

    def pass_a(c, t, qt, dst, mx):
        k = k_ref[0, kv_rows(c), pl.ds(pl.multiple_of(head_of(t) * QK_PAD, QK_PAD), QK_PAD)]
        s = _dot(k, qt)
        dst[kv_rows(c), :] = s
        return jnp.maximum(mx, jnp.max(s.reshape(tkv // sub, sub, tqs), axis=0))

    def pass_b(c, t, m, src, acc):
        p = jnp.exp2(src[kv_rows(c), :] - m).astype(BF16)
        return acc + _dot(vt_ref[0, head_of(t), :, kv_rows(c)], p)

    def col_max(mx):
        return jnp.max(mx, axis=0, keepdims=True)

    def finish(t, acc):
        o_ref[0, head_of(t), :, q_cols(t)] = (acc[:MLA_V_DIM] / acc[MLA_V_DIM:MLA_V_DIM + 1]).astype(BF16)

    def scores_only(t, dst):
        qt = qt_ref[0, head_of(t), :, q_cols(t)]
        mx = mx_init
        for c in range(n_kv):
            mx = pass_a(c, t, qt, dst, mx)
        return col_max(mx)

    def softmax_and_next_scores(t, m, src, dst):
        qt_next = qt_ref[0, head_of(t + 1), :, q_cols(t + 1)]
        acc, mx = acc_init, mx_init
        for c in range(n_kv + skew):
            if c < n_kv:
                mx = pass_a(c, t + 1, qt_next, dst, mx)
            if c >= skew:
                acc = pass_b(c - skew, t, m, src, acc)
        finish(t, acc)
        return col_max(mx)

    def softmax_only(t, m, src):
        acc = acc_init
        for c in range(n_kv):
            acc = pass_b(c, t, m, src, acc)
        finish(t, acc)

    def two_tiles(i, m):
        m = softmax_and_next_scores(2 * i, m, s0_scr, s1_scr)
        return softmax_and_next_scores(2 * i + 1, m, s1_scr, s0_scr)

    unroll = max(2, ATTN_BODY_KEYS // (2 * S))
    m = lax.fori_loop(0, n_tiles // 2 - 1, two_tiles, scores_only(0, s0_scr), unroll=unroll)
    m = softmax_and_next_scores(n_tiles - 2, m, s0_scr, s1_scr)
    softmax_only(n_tiles - 1, m, s1_scr)


def _attn_stage(qt, k, vt):
    B, H, _, S = qt.shape
    tqs = min(ATTN_TQ, S)
    tkv = max(min(ATTN_TKV, S), S // ATTN_MAX_CHUNKS)
    g = max(1,min(H, ATTN_STEP_KEYS // S))
    return pl.pallas_call(
        functools.partial(_attn_kernel, tkv=tkv, tqs=tqs, skew=ATTN_SKEW),
        grid=(B, H // g),
        in_specs=[
            pl.BlockSpec((1, g, QK_PAD, S), lambda b, h: (b, h, 0, 0)),
            pl.BlockSpec((1, S, g * QK_PAD), lambda b, h: (b, 0, h)),
            pl.BlockSpec((1, g, VT_ROWS, S), lambda b, h: (b, h, 0, 0)),
        ],
        out_specs=pl.BlockSpec((1, g, MLA_V_DIM, S), lambda b, h: (b, h, 0, 0)),
        out_shape=jax.ShapeDtypeStruct((B, H, MLA_V_DIM, S), BF16),
        scratch_shapes=[pltpu.VMEM((S, tqs), F32), pltpu.VMEM((S, tqs), F32)],
        compiler_params=pltpu.CompilerParams(
            dimension_semantics=("parallel", "parallel"), vmem_limit_bytes=VMEM_LIMIT),
        name="attn_stage",
    )(qt, k, vt)


def _ret_kernel(lgf_ref, lgb_ref, rq_ref, rk_ref, rv_ref, rg_ref, o_ref, sf_scr, sb_scr, *, unroll):
    C, Dv = CHUNK, RET_V_DIM
    W = 2 * Dv
    n_chunks = rq_ref.shape[1] // C
    pair = pl.program_id(1)

    def iota(shape, dim):
        return lax.broadcasted_iota(jnp.int32, shape, dim)

    def rate(head_idx, ref):
        return jnp.where(head_idx == 0, ref[2 * pair], ref[2 * pair + 1])

    lane_head = (iota((1, LANES), 1) // (RET_QK_DIM // 2)) % 2
    col_head = iota((1, W), 1) // Dv
    lgf_lane, lgb_lane = rate(lane_head, lgf_ref), rate(lane_head, lgb_ref)
    lgf_col, lgb_col = rate(col_head, lgf_ref), rate(col_head, lgb_ref)
    row = iota((C, LANES), 0).astype(F32)
    q_dec_f = jnp.exp(lgf_lane * (row + 1.0))
    q_dec_b = jnp.exp(lgb_lane * (C - row))
    k_dec_f = jnp.exp(lgf_lane * (C - 1.0 - row))
    k_dec_b = jnp.exp(lgb_lane * row)
    diff = (iota((C, W), 0) - iota((C, W), 1) % C).astype(F32)
    decay = jnp.where(diff >= 0, jnp.exp(lgf_col * jnp.maximum(diff, 0.0)),
                      jnp.exp(lgb_col * jnp.maximum(-diff, 0.0)))
    chunk_dec_f = jnp.exp(lgf_col * C)
    chunk_dec_b = jnp.exp(lgb_col * C)
    own = (iota((LANES, W), 0) // (RET_QK_DIM // 2)) % 2 == iota((LANES, W), 1) // Dv
    h0_lane = lane_head == 0
    h0_col = col_head == 0

    def chunk(n):
        return pl.ds(pl.multiple_of(n * C, C), C)

    def kv(n, k_dec):
        return _dot_tn((rk_ref[0, chunk(n), :].astype(F32) * k_dec).astype(BF16), rv_ref[0, chunk(n), :])

    def state_body(i, carry):
        sf, sb = carry
        fwd = [i * unroll + u for u in range(unroll)]
        bwd = [n_chunks - 1 - t for t in fwd]
        kv_f = [kv(t, k_dec_f) for t in fwd]
        kv_b = [kv(n, k_dec_b) for n in bwd]
        for u in range(unroll):
            sf_scr[fwd[u]] = sf.astype(BF16)
            sb_scr[bwd[u]] = sb.astype(BF16)
            sf = sf * chunk_dec_f + jnp.where(own, kv_f[u], 0.0)
            sb = sb * chunk_dec_b + jnp.where(own, kv_b[u], 0.0)
        return sf, sb

    zero = jnp.zeros((LANES, W), F32)
    lax.fori_loop(0, n_chunks // unroll, state_body, (zero, zero))

    def scores(n):
        k = rk_ref[0, chunk(n), :]
        k_heads = jnp.concatenate([jnp.where(h0_lane, k, jnp.zeros_like(k)),
                                   jnp.where(h0_lane, jnp.zeros_like(k), k)], axis=0)
        return _dot_nt(rq_ref[0, chunk(n), :], k_heads)

    def outputs(n, s):
        qf = rq_ref[0, chunk(n), :].astype(F32)
        v = rv_ref[0, chunk(n), :]
        lhs = jnp.concatenate([(s * decay).astype(BF16), (qf * q_dec_f).astype(BF16),
                               (qf * q_dec_b).astype(BF16)], axis=1)
        v_heads = jnp.concatenate([jnp.where(h0_col, v, jnp.zeros_like(v)),
                                   jnp.where(h0_col, jnp.zeros_like(v), v)], axis=0)
        return _dot(lhs, jnp.concatenate([v_heads, sf_scr[n], sb_scr[n]], axis=0))

    def out_body(i, carry):
        chunks = [i * unroll + u for u in range(unroll)]
        s = [scores(n) for n in chunks]
        o = [outputs(n, s_n) for n, s_n in zip(chunks, s)]
        for n, o_n in zip(chunks, o):
            normed = jnp.concatenate([_rms(o_n[:, :Dv]), _rms(o_n[:, Dv:])], axis=1)
            o_ref[0, chunk(n), :] = (normed * rg_ref[0, chunk(n), :].astype(F32)).astype(BF16)
        return carry

    lax.fori_loop(0, n_chunks // unroll, out_body, 0)


def _ret_stage(lgf, lgb, rq, rk, rv, rg):
    B, S, _ = rq.shape
    qk_spec = pl.BlockSpec((1, S, LANES), lambda b, p: (b, 0, p))
    v_spec = pl.BlockSpec((1, S, 2 * RET_V_DIM), lambda b, p: (b, 0, p))
    smem = pl.BlockSpec(memory_space=pltpu.SMEM)
    state_scr = pltpu.VMEM((S // CHUNK, LANES, 2 * RET_V_DIM), BF16)
    assert (S // CHUNK) % RET_UNROLL == 0
    return pl.pallas_call(
        functools.partial(_ret_kernel, unroll=RET_UNROLL),
        grid=(B, RET_HEADS // 2),
        in_specs=[smem, smem, qk_spec, qk_spec, v_spec, v_spec],
        out_specs=v_spec,
        out_shape=jax.ShapeDtypeStruct((B, S, RET_HEADS * RET_V_DIM), BF16),
        scratch_shapes=[state_scr, state_scr],
        compiler_params=pltpu.CompilerParams(
            dimension_semantics=("parallel", "parallel"), vmem_limit_bytes=VMEM_LIMIT),
        name="ret_stage",
    )(lgf, lgb, rq, rk, rv, rg)


def _post_kernel(x_ref, at_ref, r_ref, sa_ref, sb_ref, wa_ref, wb_ref, wo_ref, gpm_ref,
                 gmlp_ref, wup_ref, wdn_ref, gpost_ref, y_ref):
    tm = x_ref.shape[1]
    ts = tm // POST_SPLIT
    parts = [slice(i * ts, (i + 1) * ts) for i in range(POST_SPLIT)]
    a = [_dot_tn(at_ref[0, :, :, r].reshape(MLA_HEADS * MLA_V_DIM, ts), wa_ref[...]) for r in parts]
    b = [_dot(r_ref[0, r, :], wb_ref[...]) for r in parts]
    merged = [(sa_ref[0, r, :].astype(F32) * a_i + sb_ref[0, r, :].astype(F32) * b_i).astype(BF16)
              for r, a_i, b_i in zip(parts, a, b)]
    m = [_dot(g, wo_ref[...]) for g in merged]
    x1 = [x_ref[0, r, :] + _rms(m_i, gpm_ref[...]) for r, m_i in zip(parts, m)]
    h = [_rms(x_i, gmlp_ref[...]).astype(BF16) for x_i in x1]
    u = [jnp.maximum(_dot(h_i, wup_ref[...]), 0.0) for h_i in h]
    d = [_dot((u_i * u_i).astype(BF16), wdn_ref[...]) for u_i in u]
    for r, x_i, d_i in zip(parts, x1, d):
        y_ref[0, r, :] = x_i + _rms(d_i, gpost_ref[...])


def _post_stage(x, at, r, sa, sb, w):
    B, S, D = x.shape
    tm = min(TM_POST, S)
    tok = lambda n: pl.BlockSpec((1, tm, n), lambda b, i: (b, i, 0))
    consts = [w["w_branch_a"], w["w_branch_b"], w["w_out"], w["g_post_mix"],
              w["g_pre_mlp"], w["w_up"], w["w_down"], w["g_post_mlp"]]
    return pl.pallas_call(
        _post_kernel,
        grid=(B, S // tm),
        in_specs=[tok(D), pl.BlockSpec((1, MLA_HEADS, MLA_V_DIM, tm), lambda b, i: (b, 0, 0, i)),
                  tok(RET_HEADS * RET_V_DIM), tok(D), tok(D)] + [_const_spec(c.shape) for c in consts],
        out_specs=tok(D),
        out_shape=jax.ShapeDtypeStruct((B, S, D), F32),
        compiler_params=pltpu.CompilerParams(
            dimension_semantics=("parallel", "parallel"), vmem_limit_bytes=VMEM_LIMIT),
        name="post_stage",
    )(x, at, r, sa, sb, *consts)


def _prepare_weights(g_pre_mix, w_in, g_q_norm, w_q_up, g_kv_norm, w_kv_up, w_branch_a, w_branch_b,
                     w_out, g_post_mix, g_pre_mlp, w_up, w_down, g_post_mlp):
    D = D_MODEL
    splits = (Q_LORA, KV_LORA, ROPE_DIM, RET_HEADS * RET_QK_DIM, RET_HEADS * RET_QK_DIM,
              RET_HEADS * RET_V_DIM, RET_HEADS * RET_V_DIM, D, D)
    offs = np.cumsum(splits)[:-1].tolist()
    w_cq, w_ckv, w_kr, w_rq, w_rk, w_rv, w_rg, w_ga, w_gb = jnp.split(w_in, offs, axis=-1)
    half = ROPE_DIM // 2
    w_kr_sw = jnp.concatenate([w_kr[:, half:], w_kr[:, :half]], axis=1)
    w_small = jnp.concatenate([w_cq, w_ckv, w_kr, w_kr_sw, jnp.zeros((D, LANES - 2 * ROPE_DIM), F32)], axis=1)

    rh = RET_QK_DIM // 2
    perm = np.array([RET_QK_DIM * (2 * p + hh) + rh * part + r
                     for p in range(RET_HEADS // 2) for part in range(2) for hh in range(2) for r in range(rh)])

    wq = w_q_up.reshape(Q_LORA, MLA_HEADS, NOPE_DIM + ROPE_DIM)
    w_qn_t = wq[:, :, :NOPE_DIM].reshape(Q_LORA, MLA_HEADS * NOPE_DIM).T
    w_qr_t = wq[:, :, NOPE_DIM:].reshape(Q_LORA, MLA_HEADS * ROPE_DIM).T

    wkv = w_kv_up.reshape(KV_LORA, MLA_HEADS, NOPE_DIM + MLA_V_DIM)
    k_nope = jnp.pad(wkv[:, :, :NOPE_DIM], ((0, 0), (0, 0), (0, QK_PAD - NOPE_DIM)))
    rope_place = jnp.pad(jnp.eye(ROPE_DIM, dtype=F32), ((0, 0), (NOPE_DIM, QK_PAD - NOPE_DIM - ROPE_DIM)))
    rope_rows = jnp.broadcast_to(rope_place[:, None, :], (ROPE_DIM, MLA_HEADS, QK_PAD))
    w_k = jnp.concatenate(
        [k_nope, rope_rows, jnp.zeros((LANES - ROPE_DIM, MLA_HEADS, QK_PAD), F32)], axis=0
    ).reshape(KV_LORA + LANES, MLA_HEADS * QK_PAD)
    w_v_t = wkv[:, :, NOPE_DIM:].reshape(KV_LORA, MLA_HEADS * MLA_V_DIM).T

    bf = lambda a: a.astype(BF16)
    row = lambda g: g.reshape(1, -1)
    return dict(
        g_pre_mix=row(g_pre_mix), w_small=bf(w_small), w_rq=bf(w_rq[:, perm]), w_rk=bf(w_rk[:, perm]),
        w_rv=bf(w_rv), w_rg=bf(w_rg), w_ga=bf(w_ga), w_gb=bf(w_gb),
        g_q_norm=row(g_q_norm), w_qn_t=bf(w_qn_t), w_qr_t=bf(w_qr_t), g_kv_norm=row(g_kv_norm),
        w_k=bf(w_k), w_v_t=bf(w_v_t),
        w_branch_a=bf(w_branch_a), w_branch_b=bf(w_branch_b), w_out=bf(w_out), g_post_mix=row(g_post_mix),
        g_pre_mlp=row(g_pre_mlp), w_up=bf(w_up), w_down=bf(w_down), g_post_mlp=row(g_post_mlp),
    )


def _trunk(x, w, lgf, lgb, tables):
    qt, k, vt, rq, rk, rv, rg, sa, sb = _pre_stage(x, w, tables)
    at = _attn_stage(qt, k, vt)
    r = _ret_stage(lgf, lgb, rq, rk, rv, rg)
    return _post_stage(x, at, r, sa, sb, w)


def kernel(x_prompt, x_sample, g_pre_mix, w_in, g_q_norm, w_q_up, g_kv_norm, w_kv_up, w_branch_a,
           ret_log_decay_fwd, ret_log_decay_bwd, w_branch_b, w_out, g_post_mix, g_pre_mlp, w_up, w_down,
           g_post_mlp):
    assert w_in.shape[0] == 1, "single-layer trunk"
    w = _prepare_weights(g_pre_mix[0], w_in[0], g_q_norm[0], w_q_up[0], g_kv_norm[0], w_kv_up[0],
                         w_branch_a[0], w_branch_b[0], w_out[0], g_post_mix[0], g_pre_mlp[0],
                         w_up[0], w_down[0], g_post_mlp[0])
    lgf = ret_log_decay_fwd[0].astype(F32)
    lgb = ret_log_decay_bwd[0].astype(F32)
    tables = _rope_tables(max(x_prompt.shape[1], x_sample.shape[1]))
    return _trunk(x_prompt, w, lgf, lgb, tables), _trunk(x_sample, w, lgf, lgb, tables)
```

```python
import functools

import jax
import jax.numpy as jnp
import numpy as np
from jax import lax
from jax.experimental import pallas as pl
from jax.experimental.pallas import tpu as pltpu

F32 = jnp.float32
BF16 = jnp.bfloat16

D_MODEL = 1024
MLA_HEADS = 8
Q_LORA = 256
KV_LORA = 128
NOPE_DIM = 64
ROPE_DIM = 32
MLA_V_DIM = 64
RET_HEADS = 8
RET_QK_DIM = 64
RET_V_DIM = 128
CHUNK = 128
D_FF = 4 * D_MODEL
ROPE_THETA = 10000.0
EPS = 1e-6

LANES = 128
QK_PAD = LANES
SMALL_N = Q_LORA + KV_LORA + LANES
VMEM_LIMIT = 56 * 1024 * 1024

VT_ROWS = MLA_V_DIM + 16

TM_PRE = 512
PRE_SPLIT = 2
TM_POST = 512
POST_SPLIT = 2
ROPE_POS_LO = 128
ATTN_TQ = 256
ATTN_TKV = 512
ATTN_MAX_CHUNKS = 16
ATTN_BODY_KEYS = 16384
ATTN_STEP_KEYS = 16384
ATTN_SKEW = 2
RET_UNROLL = 16


def _dot(a, b):
    return jnp.dot(a, b, preferred_element_type=F32)


def _dot_nt(a, b):
    return lax.dot_general(a, b, (((1,), (1,)), ((), ())), preferred_element_type=F32)


def _dot_tn(a, b):
    return lax.dot_general(a, b, (((0,), (0,)), ((), ())), preferred_element_type=F32)


def _rms(x, g=None):
    y = x * lax.rsqrt(jnp.mean(x * x, axis=-1, keepdims=True) + EPS)
    if g is not None:
        y = y * g
    return y


def _const_spec(shape):
    nd = len(shape)
    return pl.BlockSpec(shape, lambda *_: (0,) * nd, pipeline_mode=pl.Buffered(1))


def _pre_kernel(x_ref, g_ref, wsm_ref, wrq_ref, wrk_ref, wrv_ref, wrg_ref, wga_ref, wgb_ref,
                gq_ref, wqn_ref, wqr_ref, gkv_ref, wk_ref, wv_ref,
                cos_t_ref, sin_t_ref, cos_k_ref, sin_k_ref, cos_r_ref, sin_r_ref,
                qt_ref, k_ref, vt_ref, rq_ref, rk_ref, rv_ref, rg_ref, sa_ref, sb_ref):
    tm = x_ref.shape[1]
    ts = tm // PRE_SPLIT
    parts = [slice(i * ts, (i + 1) * ts) for i in range(PRE_SPLIT)]
    each = lambda fn, *lists: [fn(*args) for args in zip(*lists)]

    h = [_rms(x_ref[0, r, :], g_ref[...]).astype(BF16) for r in parts]
    sm = each(lambda a: _dot(a, wsm_ref[...]), h)
    rq = each(lambda a: _dot(a, wrq_ref[...]), h)
    rk = each(lambda a: _dot(a, wrk_ref[...]) * (RET_QK_DIM ** -0.5), h)
    rv = each(lambda a: _dot(a, wrv_ref[...]), h)

    cqn = each(lambda a: _rms(a[:, :Q_LORA], gq_ref[...]).astype(BF16), sm)
    ckvn = each(lambda a: _rms(a[:, Q_LORA:Q_LORA + KV_LORA], gkv_ref[...]).astype(BF16), sm)
    kr = [a[:, Q_LORA + KV_LORA:] for a in sm]
    k_rope = [(t * cos_k_ref[r, :] + pltpu.roll(t, LANES - ROPE_DIM, 1) * sin_k_ref[r, :]).astype(BF16)
              for t, r in zip(kr, parts)]

    scale = (NOPE_DIM + ROPE_DIM) ** -0.5 * np.log2(np.e)
    qn = each(lambda a: _dot_nt(wqn_ref[...], a) * scale, cqn)
    qr = each(lambda a: _dot_nt(wqr_ref[...], a) * scale, cqn)
    kk = each(lambda a, b: _dot(jnp.concatenate([a, b], axis=1), wk_ref[...]), ckvn, k_rope)
    vt = each(lambda a: _dot_nt(wv_ref[...], a), ckvn)
    rg = each(lambda a: _dot(a, wrg_ref[...]), h)
    ga = each(lambda a: _dot(a, wga_ref[...]), h)
    gb = each(lambda a: _dot(a, wgb_ref[...]), h)

    half = ROPE_DIM // 2
    cos_r, sin_r = cos_r_ref[...], sin_r_ref[...]
    for i, r in enumerate(parts):
        for p in range(RET_HEADS // 2):
            sl = slice(p * LANES, (p + 1) * LANES)
            bq, bk = rq[i][:, sl], rk[i][:, sl]
            rq_ref[0, r, sl] = (bq * cos_r[r] + pltpu.roll(bq, LANES // 2, 1) * sin_r[r]).astype(BF16)
            rk_ref[0, r, sl] = (bk * cos_r[r] + pltpu.roll(bk, LANES // 2, 1) * sin_r[r]).astype(BF16)
        rv_ref[0, r, :] = rv[i].astype(BF16)

        q_rope = qr[i].reshape(MLA_HEADS, ROPE_DIM, ts)
        x1, x2 = q_rope[:, :half], q_rope[:, half:]
        cos_t, sin_t = cos_t_ref[:, r][None], sin_t_ref[:, r][None]
        qt = jnp.concatenate(
            [qn[i].reshape(MLA_HEADS, NOPE_DIM, ts), x1 * cos_t - x2 * sin_t, x1 * sin_t + x2 * cos_t,
             jnp.zeros((MLA_HEADS, QK_PAD - NOPE_DIM - ROPE_DIM, ts), F32)], axis=1)
        qt_ref[0, :, :, r] = qt.astype(BF16)
        k_ref[0, r, :] = kk[i].astype(BF16)
        ones = jnp.ones((MLA_HEADS, VT_ROWS - MLA_V_DIM, ts), F32)
        vt_ref[0, :, :, r] = jnp.concatenate([vt[i].reshape(MLA_HEADS, MLA_V_DIM, ts), ones], axis=1).astype(BF16)

        rg_ref[0, r, :] = (rg[i] * jax.nn.sigmoid(rg[i])).astype(BF16)
        sa_ref[0, r, :] = jax.nn.sigmoid(ga[i]).astype(BF16)
        sb_ref[0, r, :] = jax.nn.sigmoid(gb[i]).astype(BF16)


def _pre_stage(x, w, tables):
    B, S, D = x.shape
    tm = min(TM_PRE, S)
    grid = (B, S // tm)
    tok = lambda n: pl.BlockSpec((1, tm, n), lambda b, i: (b, i, 0))
    tab = lambda n: pl.BlockSpec((tm, n), lambda b, i: (i, 0))
    tab_t = pl.BlockSpec((ROPE_DIM // 2, tm), lambda b, i: (0, i))
    head_t = lambda n: pl.BlockSpec((1, MLA_HEADS, n, tm), lambda b, i: (b, 0, 0, i))
    consts = [w["g_pre_mix"], w["w_small"], w["w_rq"], w["w_rk"], w["w_rv"], w["w_rg"], w["w_ga"], w["w_gb"],
              w["g_q_norm"], w["w_qn_t"], w["w_qr_t"], w["g_kv_norm"], w["w_k"], w["w_v_t"]]
    cos_t, sin_t, cos_k, sin_k, cos_r, sin_r = tables
    out_shape = [
        jax.ShapeDtypeStruct((B, MLA_HEADS, QK_PAD, S), BF16),
        jax.ShapeDtypeStruct((B, S, MLA_HEADS * QK_PAD), BF16),
        jax.ShapeDtypeStruct((B, MLA_HEADS, VT_ROWS, S), BF16),
        jax.ShapeDtypeStruct((B, S, RET_HEADS * RET_QK_DIM), BF16),
        jax.ShapeDtypeStruct((B, S, RET_HEADS * RET_QK_DIM), BF16),
        jax.ShapeDtypeStruct((B, S, RET_HEADS * RET_V_DIM), BF16),
        jax.ShapeDtypeStruct((B, S, RET_HEADS * RET_V_DIM), BF16),
        jax.ShapeDtypeStruct((B, S, D), BF16),
        jax.ShapeDtypeStruct((B, S, D), BF16),
    ]
    out_specs = [head_t(QK_PAD), tok(MLA_HEADS * QK_PAD), head_t(VT_ROWS),
                 tok(RET_HEADS * RET_QK_DIM), tok(RET_HEADS * RET_QK_DIM),
                 tok(RET_HEADS * RET_V_DIM), tok(RET_HEADS * RET_V_DIM), tok(D), tok(D)]
    return pl.pallas_call(
        _pre_kernel,
        grid=grid,
        in_specs=[tok(D)] + [_const_spec(c.shape) for c in consts]
                 + [tab_t, tab_t, tab(LANES), tab(LANES), tab(LANES), tab(LANES)],
        out_specs=out_specs,
        out_shape=out_shape,
        compiler_params=pltpu.CompilerParams(
            dimension_semantics=("parallel", "parallel"), vmem_limit_bytes=VMEM_LIMIT),
        name="pre_stage",
    )(x, *consts, cos_t, sin_t, cos_k, sin_k, cos_r, sin_r)


def _rope_angles(seq, dim):
    inv = 1.0 / (ROPE_THETA ** (jnp.arange(0, dim, 2, dtype=F32) / dim))
    assert seq % ROPE_POS_LO == 0
    lo = jnp.arange(ROPE_POS_LO, dtype=F32)[:, None] * inv[None, :]
    hi = (jnp.arange(seq // ROPE_POS_LO, dtype=F32) * ROPE_POS_LO)[:, None] * inv[None, :]
    cos_lo, sin_lo, cos_hi, sin_hi = jnp.cos(lo)[None], jnp.sin(lo)[None], jnp.cos(hi)[:, None], jnp.sin(hi)[:, None]
    cos = (cos_hi * cos_lo - sin_hi * sin_lo).reshape(seq, dim // 2)
    sin = (sin_hi * cos_lo + cos_hi * sin_lo).reshape(seq, dim // 2)
    return cos, sin


def _rope_tables(S):
    cos_m, sin_m = _rope_angles(S, ROPE_DIM)
    zpad = jnp.zeros((S, LANES - ROPE_DIM), F32)
    cos_k = jnp.concatenate([cos_m, cos_m, zpad], axis=1)
    sin_k = jnp.concatenate([-sin_m, sin_m, zpad], axis=1)
    cos_r, sin_r = _rope_angles(S, RET_QK_DIM)
    cos_r4 = jnp.concatenate([cos_r] * 4, axis=1)
    sin_r4 = jnp.concatenate([-sin_r, -sin_r, sin_r, sin_r], axis=1)
    return cos_m.T, sin_m.T, cos_k, sin_k, cos_r4, sin_r4


def _attn_kernel(qt_ref, k_ref, vt_ref, o_ref, s0_scr, s1_scr, *, tkv, tqs, skew):
    S = k_ref.shape[1]
    n_kv = S // tkv
    n_q = S // tqs
    n_tiles = qt_ref.shape[1] * n_q
    assert n_tiles % 2 == 0 and n_kv >= skew
    sub = 8
    mx_init = jnp.full((sub, tqs), -jnp.inf, F32)
    acc_init = jnp.zeros((VT_ROWS, tqs), F32)

    def head_of(t):
        return t // n_q

    def q_cols(t):
        return pl.ds(pl.multiple_of((t % n_q) * tqs, tqs), tqs)

    def kv_rows(c):
        return pl.ds(c * tkv, tkv)

    def pass_a(c, t, qt, dst, mx):
        k = k_ref[0, kv_rows(c), pl.ds(pl.multiple_of(head_of(t) * QK_PAD, QK_PAD), QK_PAD)]
        s = _dot(k, qt)
        dst[kv_rows(c), :] = s
        return jnp.maximum(mx, jnp.max(s.reshape(tkv // sub, sub, tqs), axis=0))

    def pass_b(c, t, m, src, acc):
        p = jnp.exp2(src[kv_rows(c), :] - m).astype(BF16)
        return acc + _dot(vt_ref[0, head_of(t), :, kv_rows(c)], p)

    def col_max(mx):
        return jnp.max(mx, axis=0, keepdims=True)

    def finish(t, acc):
        o_ref[0, head_of(t), :, q_cols(t)] = (acc[:MLA_V_DIM] / acc[MLA_V_DIM:MLA_V_DIM + 1]).astype(BF16)

    def scores_only(t, dst):
        qt = qt_ref[0, head_of(t), :, q_cols(t)]
        mx = mx_init
        for c in range(n_kv):
            mx = pass_a(c, t, qt, dst, mx)
        return col_max(mx)

    def softmax_and_next_scores(t, m, src, dst):
        qt_next = qt_ref[0, head_of(t + 1), :, q_cols(t + 1)]
        acc, mx = acc_init, mx_init
        for c in range(n_kv + skew):
            if c < n_kv:
                mx = pass_a(c, t + 1, qt_next, dst, mx)
            if c >= skew:
                acc = pass_b(c - skew, t, m, src, acc)
        finish(t, acc)
        return col_max(mx)

    def softmax_only(t, m, src):
        acc = acc_init
        for c in range(n_kv):
            acc = pass_b(c, t, m, src, acc)
        finish(t, acc)

    def two_tiles(i, m):
        m = softmax_and_next_scores(2 * i, m, s0_scr, s1_scr)
        return softmax_and_next_scores(2 * i + 1, m, s1_scr, s0_scr)

    unroll = max(2, ATTN_BODY_KEYS // (2 * S))
    m = lax.fori_loop(0, n_tiles // 2 - 1, two_tiles, scores_only(0, s0_scr), unroll=unroll)
    m = softmax_and_next_scores(n_tiles - 2, m, s0_scr, s1_scr)
    softmax_only(n_tiles - 1, m, s1_scr)


def _attn_stage(qt, k, vt):
    B, H, _, S = qt.shape
    tqs = min(ATTN_TQ, S)
    tkv = max(min(ATTN_TKV, S), S // ATTN_MAX_CHUNKS)
    g = max(1,min(H, ATTN_STEP_KEYS // S))
    return pl.pallas_call(
        functools.partial(_attn_kernel, tkv=tkv, tqs=tqs, skew=ATTN_SKEW),
        grid=(B, H // g),
        in_specs=[
            pl.BlockSpec((1, g, QK_PAD, S), lambda b, h: (b, h, 0, 0)),
            pl.BlockSpec((1, S, g * QK_PAD), lambda b, h: (b, 0, h)),
            pl.BlockSpec((1, g, VT_ROWS, S), lambda b, h: (b, h, 0, 0)),
        ],
        out_specs=pl.BlockSpec((1, g, MLA_V_DIM, S), lambda b, h: (b, h, 0, 0)),
        out_shape=jax.ShapeDtypeStruct((B, H, MLA_V_DIM, S), BF16),
        scratch_shapes=[pltpu.VMEM((S, tqs), F32), pltpu.VMEM((S, tqs), F32)],
        compiler_params=pltpu.CompilerParams(
            dimension_semantics=("parallel", "parallel"), vmem_limit_bytes=VMEM_LIMIT),
        name="attn_stage",
    )(qt, k, vt)


def _ret_kernel(lgf_ref, lgb_ref, rq_ref, rk_ref, rv_ref, rg_ref, o_ref, sf_scr, sb_scr, *, unroll):
    C, Dv = CHUNK, RET_V_DIM
    W = 2 * Dv
    n_chunks = rq_ref.shape[1] // C
    pair = pl.program_id(1)

    def iota(shape, dim):
        return lax.broadcasted_iota(jnp.int32, shape, dim)

    def rate(head_idx, ref):
        return jnp.where(head_idx == 0, ref[2 * pair], ref[2 * pair + 1])

    lane_head = (iota((1, LANES), 1) // (RET_QK_DIM // 2)) % 2
    col_head = iota((1, W), 1) // Dv
    lgf_lane, lgb_lane = rate(lane_head, lgf_ref), rate(lane_head, lgb_ref)
    lgf_col, lgb_col = rate(col_head, lgf_ref), rate(col_head, lgb_ref)
    row = iota((C, LANES), 0).astype(F32)
    q_dec_f = jnp.exp(lgf_lane * (row + 1.0))
    q_dec_b = jnp.exp(lgb_lane * (C - row))
    k_dec_f = jnp.exp(lgf_lane * (C - 1.0 - row))
    k_dec_b = jnp.exp(lgb_lane * row)
    diff = (iota((C, W), 0) - iota((C, W), 1) % C).astype(F32)
    decay = jnp.where(diff >= 0, jnp.exp(lgf_col * jnp.maximum(diff, 0.0)),
                      jnp.exp(lgb_col * jnp.maximum(-diff, 0.0)))
    chunk_dec_f = jnp.exp(lgf_col * C)
    chunk_dec_b = jnp.exp(lgb_col * C)
    own = (iota((LANES, W), 0) // (RET_QK_DIM // 2)) % 2 == iota((LANES, W), 1) // Dv
    h0_lane = lane_head == 0
    h0_col = col_head == 0

    def chunk(n):
        return pl.ds(pl.multiple_of(n * C, C), C)

    def kv(n, k_dec):
        return _dot_tn((rk_ref[0, chunk(n), :].astype(F32) * k_dec).astype(BF16), rv_ref[0, chunk(n), :])

    def state_body(i, carry):
        sf, sb = carry
        fwd = [i * unroll + u for u in range(unroll)]
        bwd = [n_chunks - 1 - t for t in fwd]
        kv_f = [kv(t, k_dec_f) for t in fwd]
        kv_b = [kv(n, k_dec_b) for n in bwd]
        for u in range(unroll):
            sf_scr[fwd[u]] = sf.astype(BF16)
            sb_scr[bwd[u]] = sb.astype(BF16)
            sf = sf * chunk_dec_f + jnp.where(own, kv_f[u], 0.0)
            sb = sb * chunk_dec_b + jnp.where(own, kv_b[u], 0.0)
        return sf, sb

    zero = jnp.zeros((LANES, W), F32)
    lax.fori_loop(0, n_chunks // unroll, state_body, (zero, zero))

    def scores(n):
        k = rk_ref[0, chunk(n), :]
        k_heads = jnp.concatenate([jnp.where(h0_lane, k, jnp.zeros_like(k)),
                                   jnp.where(h0_lane, jnp.zeros_like(k), k)], axis=0)
        return _dot_nt(rq_ref[0, chunk(n), :], k_heads)

    def outputs(n, s):
        qf = rq_ref[0, chunk(n), :].astype(F32)
        v = rv_ref[0, chunk(n), :]
        lhs = jnp.concatenate([(s * decay).astype(BF16), (qf * q_dec_f).astype(BF16),
                               (qf * q_dec_b).astype(BF16)], axis=1)
        v_heads = jnp.concatenate([jnp.where(h0_col, v, jnp.zeros_like(v)),
                                   jnp.where(h0_col, jnp.zeros_like(v), v)], axis=0)
        return _dot(lhs, jnp.concatenate([v_heads, sf_scr[n], sb_scr[n]], axis=0))

    def out_body(i, carry):
        chunks = [i * unroll + u for u in range(unroll)]
        s = [scores(n) for n in chunks]
        o = [outputs(n, s_n) for n, s_n in zip(chunks, s)]
        for n, o_n in zip(chunks, o):
            normed = jnp.concatenate([_rms(o_n[:, :Dv]), _rms(o_n[:, Dv:])], axis=1)
            o_ref[0, chunk(n), :] = (normed * rg_ref[0, chunk(n), :].astype(F32)).astype(BF16)
        return carry

    lax.fori_loop(0, n_chunks // unroll, out_body, 0)


def _ret_stage(lgf, lgb, rq, rk, rv, rg):
    B, S, _ = rq.shape
    qk_spec = pl.BlockSpec((1, S, LANES), lambda b, p: (b, 0, p))
    v_spec = pl.BlockSpec((1, S, 2 * RET_V_DIM), lambda b, p: (b, 0, p))
    smem = pl.BlockSpec(memory_space=pltpu.SMEM)
    state_scr = pltpu.VMEM((S // CHUNK, LANES, 2 * RET_V_DIM), BF16)
    assert (S // CHUNK) % RET_UNROLL == 0
    return pl.pallas_call(
        functools.partial(_ret_kernel, unroll=RET_UNROLL),
        grid=(B, RET_HEADS // 2),
        in_specs=[smem, smem, qk_spec, qk_spec, v_spec, v_spec],
        out_specs=v_spec,
        out_shape=jax.ShapeDtypeStruct((B, S, RET_HEADS * RET_V_DIM), BF16),
        scratch_shapes=[state_scr, state_scr],
        compiler_params=pltpu.CompilerParams(
            dimension_semantics=("parallel", "parallel"), vmem_limit_bytes=VMEM_LIMIT),
        name="ret_stage",
    )(lgf, lgb, rq, rk, rv, rg)


def _post_kernel(x_ref, at_ref, r_ref, sa_ref, sb_ref, wa_ref, wb_ref, wo_ref, gpm_ref,
                 gmlp_ref, wup_ref, wdn_ref, gpost_ref, y_ref):
    tm = x_ref.shape[1]
    ts = tm // POST_SPLIT
    parts = [slice(i * ts, (i + 1) * ts) for i in range(POST_SPLIT)]
    a = [_dot_tn(at_ref[0, :, :, r].reshape(MLA_HEADS * MLA_V_DIM, ts), wa_ref[...]) for r in parts]
    b = [_dot(r_ref[0, r, :], wb_ref[...]) for r in parts]
    merged = [(sa_ref[0, r, :].astype(F32) * a_i + sb_ref[0, r, :].astype(F32) * b_i).astype(BF16)
              for r, a_i, b_i in zip(parts, a, b)]
    m = [_dot(g, wo_ref[...]) for g in merged]
    x1 = [x_ref[0, r, :] + _rms(m_i, gpm_ref[...]) for r, m_i in zip(parts, m)]
    h = [_rms(x_i, gmlp_ref[...]).astype(BF16) for x_i in x1]
    u = [jnp.maximum(_dot(h_i, wup_ref[...]), 0.0) for h_i in h]
    d = [_dot((u_i * u_i).astype(BF16), wdn_ref[...]) for u_i in u]
    for r, x_i, d_i in zip(parts, x1, d):
        y_ref[0, r, :] = x_i + _rms(d_i, gpost_ref[...])


def _post_stage(x, at, r, sa, sb, w):
    B, S, D = x.shape
    tm = min(TM_POST, S)
    tok = lambda n: pl.BlockSpec((1, tm, n), lambda b, i: (b, i, 0))
    consts = [w["w_branch_a"], w["w_branch_b"], w["w_out"], w["g_post_mix"],
              w["g_pre_mlp"], w["w_up"], w["w_down"], w["g_post_mlp"]]
    return pl.pallas_call(
        _post_kernel,
        grid=(B, S // tm),
        in_specs=[tok(D), pl.BlockSpec((1, MLA_HEADS, MLA_V_DIM, tm), lambda b, i: (b, 0, 0, i)),
                  tok(RET_HEADS * RET_V_DIM), tok(D), tok(D)] + [_const_spec(c.shape) for c in consts],
        out_specs=tok(D),
        out_shape=jax.ShapeDtypeStruct((B, S, D), F32),
        compiler_params=pltpu.CompilerParams(
            dimension_semantics=("parallel", "parallel"), vmem_limit_bytes=VMEM_LIMIT),
        name="post_stage",
    )(x, at, r, sa, sb, *consts)


def _prepare_weights(g_pre_mix, w_in, g_q_norm, w_q_up, g_kv_norm, w_kv_up, w_branch_a, w_branch_b,
                     w_out, g_post_mix, g_pre_mlp, w_up, w_down, g_post_mlp):
    D = D_MODEL
    splits = (Q_LORA, KV_LORA, ROPE_DIM, RET_HEADS * RET_QK_DIM, RET_HEADS * RET_QK_DIM,
              RET_HEADS * RET_V_DIM, RET_HEADS * RET_V_DIM, D, D)
    offs = np.cumsum(splits)[:-1].tolist()
    w_cq, w_ckv, w_kr, w_rq, w_rk, w_rv, w_rg, w_ga, w_gb = jnp.split(w_in, offs, axis=-1)
    half = ROPE_DIM // 2
    w_kr_sw = jnp.concatenate([w_kr[:, half:], w_kr[:, :half]], axis=1)
    w_small = jnp.concatenate([w_cq, w_ckv, w_kr, w_kr_sw, jnp.zeros((D, LANES - 2 * ROPE_DIM), F32)], axis=1)

    rh = RET_QK_DIM // 2
    perm = np.array([RET_QK_DIM * (2 * p + hh) + rh * part + r
                     for p in range(RET_HEADS // 2) for part in range(2) for hh in range(2) for r in range(rh)])

    wq = w_q_up.reshape(Q_LORA, MLA_HEADS, NOPE_DIM + ROPE_DIM)
    w_qn_t = wq[:, :, :NOPE_DIM].reshape(Q_LORA, MLA_HEADS * NOPE_DIM).T
    w_qr_t = wq[:, :, NOPE_DIM:].reshape(Q_LORA, MLA_HEADS * ROPE_DIM).T

    wkv = w_kv_up.reshape(KV_LORA, MLA_HEADS, NOPE_DIM + MLA_V_DIM)
    k_nope = jnp.pad(wkv[:, :, :NOPE_DIM], ((0, 0), (0, 0), (0, QK_PAD - NOPE_DIM)))
    rope_place = jnp.pad(jnp.eye(ROPE_DIM, dtype=F32), ((0, 0), (NOPE_DIM, QK_PAD - NOPE_DIM - ROPE_DIM)))
    rope_rows = jnp.broadcast_to(rope_place[:, None, :], (ROPE_DIM, MLA_HEADS, QK_PAD))
    w_k = jnp.concatenate(
        [k_nope, rope_rows, jnp.zeros((LANES - ROPE_DIM, MLA_HEADS, QK_PAD), F32)], axis=0
    ).reshape(KV_LORA + LANES, MLA_HEADS * QK_PAD)
    w_v_t = wkv[:, :, NOPE_DIM:].reshape(KV_LORA, MLA_HEADS * MLA_V_DIM).T

    bf = lambda a: a.astype(BF16)
    row = lambda g: g.reshape(1, -1)
    return dict(
        g_pre_mix=row(g_pre_mix), w_small=bf(w_small), w_rq=bf(w_rq[:, perm]), w_rk=bf(w_rk[:, perm]),
        w_rv=bf(w_rv), w_rg=bf(w_rg), w_ga=bf(w_ga), w_gb=bf(w_gb),
        g_q_norm=row(g_q_norm), w_qn_t=bf(w_qn_t), w_qr_t=bf(w_qr_t), g_kv_norm=row(g_kv_norm),
        w_k=bf(w_k), w_v_t=bf(w_v_t),
        w_branch_a=bf(w_branch_a), w_branch_b=bf(w_branch_b), w_out=bf(w_out), g_post_mix=row(g_post_mix),
        g_pre_mlp=row(g_pre_mlp), w_up=bf(w_up), w_down=bf(w_down), g_post_mlp=row(g_post_mlp),
    )


def _trunk(x, w, lgf, lgb, tables):
    qt, k, vt, rq, rk, rv, rg, sa, sb = _pre_stage(x, w, tables)
    at = _attn_stage(qt, k, vt)
    r = _ret_stage(lgf, lgb, rq, rk, rv, rg)
    return _post_stage(x, at, r, sa, sb, w)


def kernel(x_prompt, x_sample, g_pre_mix, w_in, g_q_norm, w_q_up, g_kv_norm, w_kv_up, w_branch_a,
           ret_log_decay_fwd, ret_log_decay_bwd, w_branch_b, w_out, g_post_mix, g_pre_mlp, w_up, w_down,
           g_post_mlp):
    assert w_in.shape[0] == 1, "single-layer trunk"
    w = _prepare_weights(g_pre_mix[0], w_in[0], g_q_norm[0], w_q_up[0], g_kv_norm[0], w_kv_up[0],
                         w_branch_a[0], w_branch_b[0], w_out[0], g_post_mix[0], g_pre_mlp[0],
                         w_up[0], w_down[0], g_post_mlp[0])
    lgf = ret_log_decay_fwd[0].astype(F32)
    lgb = ret_log_decay_bwd[0].astype(F32)
    tables = _rope_tables(max(x_prompt.shape[1], x_sample.shape[1]))
    return _trunk(x_prompt, w, lgf, lgb, tables), _trunk(x_sample, w, lgf, lgb, tables)
```
